```python
import math, functools
import jax, jax.numpy as jnp
from jax import lax
import numpy as np

D_MODEL = 1024
BATCH = 4
SEQ = 4096
DEPTH = 4
DEC_BATCH = 128
DEC_SEQ = 1
PAST_LEN = 2048
PAGE_SIZE = 128

SSD_EXPAND = 2
D_INNER = SSD_EXPAND * D_MODEL
SSD_HEAD_DIM = 64
N_SSD_HEADS = D_INNER // SSD_HEAD_DIM
D_STATE = 128
N_SSD_GROUPS = 8
HEADS_PER_SSD_GROUP = N_SSD_HEADS // N_SSD_GROUPS
CONV_W = 4
CONV_CH = D_INNER + 2 * N_SSD_GROUPS * D_STATE
SSD_CHUNK = 128

DILATION_GROUPS = ((128, 1), (512, 4), (2048, 16))
N_DIL = len(DILATION_GROUPS)
ATTN_HEAD_DIM = 64
HPG = (3 * D_MODEL // 4) // ATTN_HEAD_DIM
N_ATTN_HEADS = N_DIL * HPG
ATTN_QKV = N_ATTN_HEADS * ATTN_HEAD_DIM
ATTN_OUT = HPG * ATTN_HEAD_DIM
ATTN_SCALE = ATTN_HEAD_DIM ** -0.5
N_BUCKETS = 32
MAX_DISTANCE = 2048
RMS_EPS = 1e-6

IN_SIZES = (D_INNER, CONV_CH, N_SSD_HEADS, ATTN_QKV, ATTN_QKV, ATTN_QKV, ATTN_OUT, D_MODEL, D_MODEL)
IN_SPLITS = tuple(int(s) for s in np.cumsum(IN_SIZES)[:-1])
N_IN_COLS = sum(IN_SIZES)

kernel_name = 'ssd_dilated_attn_hybrid_step'


def rmsnorm(x, g):
    xf = x.astype(jnp.float32)
    y = xf * lax.rsqrt(jnp.mean(xf * xf, axis=-1, keepdims=True) + RMS_EPS)
    return (y * g.astype(jnp.float32)).astype(x.dtype)


def t5_bucket(dist):
    max_exact = N_BUCKETS // 2
    n = jnp.maximum(dist, 1).astype(jnp.float32)
    large = max_exact + (jnp.log(n / max_exact) / math.log(MAX_DISTANCE / max_exact)
                         * (N_BUCKETS - max_exact)).astype(jnp.int32)
    large = jnp.minimum(large, N_BUCKETS - 1)
    return jnp.where(dist < max_exact, dist, large)


def group_bias(rel_bias, g):
    window, dil = DILATION_GROUPS[g]
    dist = jnp.arange(window // dil + 1, dtype=jnp.int32) * dil
    return rel_bias[t5_bucket(dist), g * HPG:(g + 1) * HPG].astype(jnp.float32)


def dilated_attn_prompt(q, k, v, bias, dil, band):
    b, s, h, e = q.shape
    n_res = s // dil
    nb = -(-n_res // band)
    lp = nb * band

    def to_res(u):
        return u.reshape(b, n_res, dil, h, e).transpose(0, 2, 1, 3, 4)

    qb = jnp.pad(to_res(q), ((0, 0), (0, 0), (0, lp - n_res), (0, 0), (0, 0))).reshape(b, dil, nb, band, h, e)

    def kv_blocks(u):
        ub = jnp.pad(to_res(u), ((0, 0), (0, 0), (band, lp - n_res), (0, 0), (0, 0))).reshape(b, dil, nb + 1, band, h, e)
        return jnp.concatenate([ub[:, :, :-1], ub[:, :, 1:]], axis=3)

    kb, vb = kv_blocks(k), kv_blocks(v)
    qi = jnp.arange(band)[:, None]
    kj = jnp.arange(2 * band)[None, :]
    diff = qi + band - kj
    in_band = (diff >= 0) & (diff <= band)
    bias_qk = jnp.transpose(bias[jnp.clip(diff, 0, band)], (2, 0, 1))
    key_idx = jnp.arange(nb)[:, None] * band - band + kj
    mask = in_band[None] & (key_idx >= 0)[:, None, :]
    scores = jnp.einsum('brnqhe,brnkhe->brnhqk', qb, kb, preferred_element_type=jnp.float32) * ATTN_SCALE
    scores = jnp.where(mask[None, None, :, None], scores + bias_qk[None, None, None], -jnp.inf)
    lse = jax.nn.logsumexp(scores, axis=-1)
    p = jnp.exp(scores - lse[..., None]).astype(v.dtype)
    o = jnp.einsum('brnhqk,brnkhe->brnqhe', p, vb)
    o = o.reshape(b, dil, lp, h, e)[:, :, :n_res].transpose(0, 2, 1, 3, 4).reshape(b, s, h, e)
    lse = lse.transpose(0, 1, 2, 4, 3).reshape(b, dil, lp, h)[:, :, :n_res].transpose(0, 2, 1, 3).reshape(b, s, h)
    return o, lse


def dilated_attn_sample(q, k_new, v_new, k_buf, v_buf, bias, dil, band):
    wc = k_buf.shape[1]
    t = q.shape[1]
    kc = jnp.concatenate([k_buf, k_new.astype(k_buf.dtype)], axis=1)
    vc = jnp.concatenate([v_buf, v_new.astype(v_buf.dtype)], axis=1)
    idx = wc + jnp.arange(t)[:, None] - jnp.arange(band + 1)[None, :] * dil
    valid = idx >= 0
    idx_c = jnp.maximum(idx, 0)
    kg = kc[:, idx_c]
    vg = vc[:, idx_c]
    scores = jnp.einsum('bthe,btkhe->bhtk', q, kg, preferred_element_type=jnp.float32) * ATTN_SCALE
    scores = jnp.where(valid[None, None], scores + bias.T[:, None, :], -jnp.inf)
    lse = jax.nn.logsumexp(scores, axis=-1)
    p = jnp.exp(scores - lse[..., None]).astype(vg.dtype)
    o = jnp.einsum('bhtk,btkhe->bthe', p, vg)
    return o, lse.transpose(0, 2, 1)


def combine_groups(outs, lses):
    w = jax.nn.softmax(jnp.stack(lses, axis=0), axis=0)
    o = jnp.einsum('gbth,gbthe->bthe', w, jnp.stack(outs, axis=0).astype(jnp.float32))
    return o.astype(outs[0].dtype)


def attn_prompt(q, k, v, rel_bias):
    s = q.shape[1]
    outs, lses, rows = [], [], []
    for g, (window, dil) in enumerate(DILATION_GROUPS):
        hs = slice(g * HPG, (g + 1) * HPG)
        o, lse = dilated_attn_prompt(q[:, :, hs], k[:, :, hs], v[:, :, hs], group_bias(rel_bias, g), dil, window // dil)
        outs.append(o)
        lses.append(lse)
        keep = min(window, s)
        rows.append(jnp.stack([k[:, s - keep:, hs], v[:, s - keep:, hs]], axis=2))
    return combine_groups(outs, lses), rows


def attn_sample(q, k, v, rel_bias, kv_bufs):
    outs, lses, rows = [], [], []
    for g, (window, dil) in enumerate(DILATION_GROUPS):
        hs = slice(g * HPG, (g + 1) * HPG)
        buf = kv_bufs[g]
        o, lse = dilated_attn_sample(q[:, :, hs], k[:, :, hs], v[:, :, hs], buf[:, :, 0], buf[:, :, 1],
                                     group_bias(rel_bias, g), dil, window // dil)
        outs.append(o)
        lses.append(lse)
        rows.append(jnp.stack([k[:, :, hs], v[:, :, hs]], axis=2))
    return combine_groups(outs, lses), rows


def ssd_scan(x, dt, a, bm, cm, init_state, chunk):
    b, t, h, p = x.shape
    nc = t // chunk
    G, R, N = N_SSD_GROUPS, HEADS_PER_SSD_GROUP, D_STATE
    f32 = jnp.float32
    xc = x.astype(f32).reshape(b, nc, chunk, G, R, p)
    dtc = dt.reshape(b, nc, chunk, G, R)
    bc = bm.astype(f32).reshape(b, nc, chunk, G, N)
    cc = cm.astype(f32).reshape(b, nc, chunk, G, N)
    acum = jnp.cumsum(dtc * a.reshape(G, R), axis=2)
    xdt = xc * dtc[..., None]
    causal = jnp.tril(jnp.ones((chunk, chunk), dtype=bool))
    seg = acum[:, :, :, None] - acum[:, :, None, :]
    decay = jnp.exp(jnp.where(causal[:, :, None, None], seg, -jnp.inf))
    cb = jnp.einsum('bcign,bcjgn->bcijg', cc, bc)
    y_diag = jnp.einsum('bcijg,bcijgr,bcjgrp->bcigrp', cb, decay, xdt)
    decay_end = jnp.exp(acum[:, :, -1:] - acum)
    states = jnp.einsum('bcjgn,bcjgr,bcjgrp->bcgrpn', bc, decay_end, xdt)
    chunk_decay = jnp.exp(acum[:, :, -1])

    def step(s, inp):
        st, cd = inp
        return s * cd[..., None, None] + st, s

    s0 = init_state.astype(f32).reshape(b, G, R, p, N)
    final, s_in = lax.scan(step, s0, (jnp.moveaxis(states, 1, 0), jnp.moveaxis(chunk_decay, 1, 0)))
    s_in = jnp.moveaxis(s_in, 0, 1)
    y_off = jnp.einsum('bcign,bcgrpn,bcigr->bcigrp', cc, s_in, jnp.exp(acum))
    y = (y_diag + y_off).reshape(b, t, h, p)
    return y, final.reshape(b, h, p, N)


def ssd_branch(z, xbc, dt_raw, conv_state, ssm_state, chunk, conv_w, conv_b, dt_bias, a_log, d_skip, g_norm):
    b, t, _ = xbc.shape
    f32 = jnp.float32
    xpad = jnp.concatenate([conv_state.astype(xbc.dtype), xbc], axis=1)
    new_conv = xpad[:, xpad.shape[1] - (CONV_W - 1):]
    xbc = lax.conv_general_dilated(xpad, conv_w[:, None, :].astype(xpad.dtype), (1,), 'VALID',
                                   dimension_numbers=('NWC', 'WIO', 'NWC'), feature_group_count=CONV_CH)
    xbc = jax.nn.silu(xbc + conv_b)
    xs, bm, cm = jnp.split(xbc, (D_INNER, D_INNER + N_SSD_GROUPS * D_STATE), axis=-1)
    xs = xs.reshape(b, t, N_SSD_HEADS, SSD_HEAD_DIM)
    bm = bm.reshape(b, t, N_SSD_GROUPS, D_STATE)
    cm = cm.reshape(b, t, N_SSD_GROUPS, D_STATE)
    dt = jax.nn.softplus(dt_raw.astype(f32) + dt_bias.astype(f32))
    a = -jnp.exp(a_log.astype(f32))
    y, final = ssd_scan(xs, dt, a, bm, cm, ssm_state, chunk)
    y = y + d_skip.astype(f32)[:, None] * xs.astype(f32)
    gshape = (b, t, N_SSD_GROUPS, D_INNER // N_SSD_GROUPS)
    yz = y.reshape(gshape) * jax.nn.silu(z.astype(f32)).reshape(gshape)
    yz = yz * lax.rsqrt(jnp.mean(yz * yz, axis=-1, keepdims=True) + RMS_EPS)
    y = yz.reshape(b, t, D_INNER) * g_norm.astype(f32)
    return y.astype(z.dtype), new_conv, final.astype(ssm_state.dtype)


def trunk_layer(x, c, attn_fn, conv_state, ssm_state, chunk, w_ada, b_ada, g_pre, w_in, conv_w, conv_b,
                dt_bias, a_log, d_skip, g_ssd_norm, w_br_ssd, w_br_attn, w_out, g_post):
    b, t, _ = x.shape
    mod = jax.nn.silu(c) @ w_ada + b_ada
    shift, scale, gate = jnp.split(mod[:, None, :], 3, axis=-1)
    h = rmsnorm(x, g_pre) * (1 + scale) + shift
    z, xbc, dt_raw, q, k, v, g_attn, m_ssd, m_attn = jnp.split(h @ w_in, IN_SPLITS, axis=-1)
    y_ssd, new_conv, new_ssm = ssd_branch(z, xbc, dt_raw, conv_state, ssm_state, chunk, conv_w, conv_b,
                                          dt_bias, a_log, d_skip, g_ssd_norm)
    heads = lambda u: u.reshape(b, t, N_ATTN_HEADS, ATTN_HEAD_DIM)
    o_attn, kv_rows = attn_fn(heads(q), heads(k), heads(v))
    o_attn = o_attn.reshape(b, t, ATTN_OUT) * jax.nn.silu(g_attn)
    merged = jax.nn.sigmoid(m_ssd) * (y_ssd @ w_br_ssd) + jax.nn.sigmoid(m_attn) * (o_attn @ w_br_attn)
    out = merged @ w_out
    return x + gate * rmsnorm(out, g_post), kv_rows, new_conv, new_ssm


def setup_inputs(seed: int = 0) -> dict:
    key = jax.random.key(seed)
    ks = jax.random.split(key, 24)
    f32 = jnp.float32

    def nrm(k, shape, scale=1.0):
        return scale * jax.random.normal(k, shape, f32)

    kv_len = [min(w, PAST_LEN) for w, _ in DILATION_GROUPS]
    dt0 = jnp.exp(jax.random.uniform(ks[19], (DEPTH, N_SSD_HEADS), f32, math.log(1e-3), math.log(1e-1)))
    return {
        'x_prompt': nrm(ks[0], (BATCH, SEQ, D_MODEL)),
        'x_sample': nrm(ks[1], (DEC_BATCH, DEC_SEQ, D_MODEL)),
        'cache_kv_w128': nrm(ks[2], (DEPTH, DEC_BATCH, kv_len[0], 2, HPG, ATTN_HEAD_DIM)),
        'cache_kv_w512': nrm(ks[3], (DEPTH, DEC_BATCH, kv_len[1], 2, HPG, ATTN_HEAD_DIM)),
        'cache_kv_w2048': nrm(ks[4], (DEPTH, DEC_BATCH, kv_len[2], 2, HPG, ATTN_HEAD_DIM)),
        'state_conv': nrm(ks[5], (DEPTH, DEC_BATCH, CONV_W - 1, CONV_CH)),
        'state_ssm': nrm(ks[6], (DEPTH, DEC_BATCH, N_SSD_HEADS, SSD_HEAD_DIM, D_STATE), 0.5),
        'c_prompt': nrm(ks[7], (BATCH, D_MODEL)),
        'c_sample': nrm(ks[8], (DEC_BATCH, D_MODEL)),
        'rel_bias': nrm(ks[9], (N_BUCKETS, N_ATTN_HEADS), 0.5),
        'w_ada': nrm(ks[10], (DEPTH, D_MODEL, 3 * D_MODEL), D_MODEL ** -0.5),
        'b_ada': nrm(ks[11], (DEPTH, 3 * D_MODEL), 0.02),
        'g_pre': 1.0 + nrm(ks[12], (DEPTH, D_MODEL), 0.1),
        'w_in': nrm(ks[13], (DEPTH, D_MODEL, N_IN_COLS), D_MODEL ** -0.5),
        'conv_w': nrm(ks[14], (DEPTH, CONV_W, CONV_CH), CONV_W ** -0.5),
        'conv_b': nrm(ks[15], (DEPTH, CONV_CH), 0.02),
        'dt_bias': dt0 + jnp.log(-jnp.expm1(-dt0)),
        'a_log': jnp.log(jax.random.uniform(ks[16], (DEPTH, N_SSD_HEADS), f32, 1.0, 16.0)),
        'd_skip': 1.0 + nrm(ks[17], (DEPTH, N_SSD_HEADS), 0.1),
        'g_ssd_norm': 1.0 + nrm(ks[18], (DEPTH, D_INNER), 0.1),
        'w_br_ssd': nrm(ks[20], (DEPTH, D_INNER, D_MODEL), D_INNER ** -0.5),
        'w_br_attn': nrm(ks[21], (DEPTH, ATTN_OUT, D_MODEL), ATTN_OUT ** -0.5),
        'w_out': nrm(ks[22], (DEPTH, D_MODEL, D_MODEL), D_MODEL ** -0.5),
        'g_post': 1.0 + nrm(ks[23], (DEPTH, D_MODEL), 0.1),
    }


def reference(x_prompt, x_sample, cache_kv_w128, cache_kv_w512, cache_kv_w2048, state_conv, state_ssm,
              c_prompt, c_sample, rel_bias, w_ada, b_ada, g_pre, w_in, conv_w, conv_b, dt_bias, a_log,
              d_skip, g_ssd_norm, w_br_ssd, w_br_attn, w_out, g_post):
    xp, xs = x_prompt, x_sample
    bp = x_prompt.shape[0]
    conv0 = jnp.zeros((bp, CONV_W - 1, CONV_CH), x_prompt.dtype)
    ssm0 = jnp.zeros((bp, N_SSD_HEADS, SSD_HEAD_DIM, D_STATE), x_prompt.dtype)
    prompt_chunk = min(SSD_CHUNK, x_prompt.shape[1])
    sample_chunk = x_sample.shape[1]
    kvp = [[], [], []]
    kvs = [[], [], []]
    convp, convs, ssmp, ssms = [], [], [], []
    for l in range(DEPTH):
        lw = (w_ada[l], b_ada[l], g_pre[l], w_in[l], conv_w[l], conv_b[l], dt_bias[l], a_log[l],
              d_skip[l], g_ssd_norm[l], w_br_ssd[l], w_br_attn[l], w_out[l], g_post[l])
        xp, rows_p, cp, sp = trunk_layer(xp, c_prompt, functools.partial(attn_prompt, rel_bias=rel_bias),
                                         conv0, ssm0, prompt_chunk, *lw)
        bufs = (cache_kv_w128[l], cache_kv_w512[l], cache_kv_w2048[l])
        xs, rows_s, cs, ss = trunk_layer(xs, c_sample, functools.partial(attn_sample, rel_bias=rel_bias, kv_bufs=bufs),
                                         state_conv[l], state_ssm[l], sample_chunk, *lw)
        for g in range(N_DIL):
            kvp[g].append(rows_p[g])
            kvs[g].append(rows_s[g])
        convp.append(cp)
        convs.append(cs)
        ssmp.append(sp)
        ssms.append(ss)
    st = lambda lst: jnp.stack(lst, axis=0)
    return (xp, xs, st(kvp[0]), st(kvs[0]), st(kvp[1]), st(kvs[1]), st(kvp[2]), st(kvs[2]),
            st(convp), st(convs), st(ssmp), st(ssms))
```

```python
import functools
import math

import numpy as np
import jax
import jax.numpy as jnp
from jax import lax
from jax.experimental import pallas as pl
from jax.experimental.pallas import tpu as pltpu

F32 = jnp.float32
BF16 = jnp.bfloat16

D_MODEL = 1024
D_INNER = 2048
SSD_HEAD_DIM = 64
N_SSD_HEADS = 32
D_STATE = 128
N_SSD_GROUPS = 8
HEADS_PER_SSD_GROUP = 4
GROUP_CH = D_INNER // N_SSD_GROUPS
CONV_W = 4
CONV_CH = 4096
SSD_CHUNK = 128
DILATION_GROUPS = ((128, 1), (512, 4), (2048, 16))
N_DIL = 3
ATTN_HEAD_DIM = 64
HPG = 12
ATTN_QKV = 2304
ATTN_OUT = 768
ATTN_SCALE = ATTN_HEAD_DIM ** -0.5
N_BUCKETS = 32
MAX_DISTANCE = 2048
RMS_EPS = 1e-6
BAND = 128
MASKED = -1e30

LANES = 128
SUBLANES = 8
VMEM_LIMIT = 56 * 1024 * 1024

COL_XS = 0
COL_Z = 2048
COL_B = 4096
COL_C = 5120
COL_Q = 6144
COL_K = 8448
COL_V = 10752
COL_GA = 13056
COL_MS = 13824
COL_MA = 14848
COL_DT = 15872
N_COLS = 16128
QKV_BLOCKS = N_COLS // ATTN_OUT


def _dot(a, b):
    return jnp.dot(a, b, preferred_element_type=F32)


def _dot_nt(a, b):
    return lax.dot_general(a, b, (((1,), (1,)), ((), ())), preferred_element_type=F32)


def _split3(x):
    x1 = x.astype(BF16)
    r1 = x - x1.astype(F32)
    x2 = r1.astype(BF16)
    x3 = (r1 - x2.astype(F32)).astype(BF16)
    return x1, x2, x3


def _dot_sel_rhs(x, sel):
    x1, x2, x3 = _split3(x)
    return _dot(x1, sel) + _dot(x2, sel) + _dot(x3, sel)


def _dot_sel_lhs(sel, x):
    x1, x2, x3 = _split3(x)
    return _dot(sel, x1) + _dot(sel, x2) + _dot(sel, x3)


def _silu(x):
    return x * jax.nn.sigmoid(x)


def _softplus(x):
    return jnp.maximum(x, 0.0) + jnp.log1p(jnp.exp(-jnp.abs(x)))


def _params(*sem):
    return pltpu.CompilerParams(dimension_semantics=sem, vmem_limit_bytes=VMEM_LIMIT)


def _ada_kernel(c_ref, w_ref, b_ref, o_ref):
    c = c_ref[...]
    o_ref[...] = _dot(_silu(c).astype(BF16), w_ref[...].astype(BF16)) + b_ref[...]


def _ada(c_all, w_ada, b_ada):
    depth = w_ada.shape[0]
    rows = c_all.shape[0]
    nj = 3 * D_MODEL // 1024
    return pl.pallas_call(
        _ada_kernel,
        grid=(depth, nj),
        in_specs=[
            pl.BlockSpec((rows, D_MODEL), lambda l, j: (0, 0)),
            pl.BlockSpec((None, D_MODEL, 1024), lambda l, j: (l, 0, j)),
            pl.BlockSpec((None, 1, 1024), lambda l, j: (l, 0, j)),
        ],
        out_specs=pl.BlockSpec((None, rows, 1024), lambda l, j: (l, 0, j)),
        out_shape=jax.ShapeDtypeStruct((depth, rows, 3 * D_MODEL), F32),
        compiler_params=_params("parallel", "parallel"),
        name="ada",
    )(c_all, w_ada, b_ada.reshape(depth, 1, 3 * D_MODEL))


def _inproj_kernel(x_ref, mod_ref, g_ref, w_ref, o_ref, h_ref):
    @pl.when(pl.program_id(2) == 0)
    def _():
        x = x_ref[...]
        ms = jnp.mean(x * x, axis=-1, keepdims=True)
        y = x * lax.rsqrt(ms + RMS_EPS) * g_ref[...]
        shift = mod_ref[:, 0:D_MODEL]
        scale = mod_ref[:, D_MODEL:2 * D_MODEL]
        h_ref[...] = (y * (1.0 + scale) + shift).astype(BF16)

    o_ref[...] = _dot(h_ref[...], w_ref[...])


def _inproj(x, mod, g_pre, w_bf16, tm, tn):
    b, t, _ = x.shape
    r = mod.shape[1]
    rb = 1 if r == 1 else tm
    mod_map = (lambda bi, i, j: (bi, 0, 0)) if r == 1 else (lambda bi, i, j: (bi, i, 0))
    return pl.pallas_call(
        _inproj_kernel,
        grid=(b, t // tm, N_COLS // tn),
        in_specs=[
            pl.BlockSpec((None, tm, D_MODEL), lambda bi, i, j: (bi, i, 0)),
            pl.BlockSpec((None, rb, 3 * D_MODEL), mod_map),
            pl.BlockSpec((1, D_MODEL), lambda bi, i, j: (0, 0)),
            pl.BlockSpec((D_MODEL, tn), lambda bi, i, j: (0, j)),
        ],
        out_specs=pl.BlockSpec((None, tm, tn), lambda bi, i, j: (bi, i, j)),
        out_shape=jax.ShapeDtypeStruct((b, t, N_COLS), F32),
        scratch_shapes=[pltpu.VMEM((tm, D_MODEL), BF16)],
        compiler_params=_params("parallel", "parallel", "arbitrary"),
        name="inproj",
    )(x, mod, g_pre.reshape(1, D_MODEL), w_bf16)


def _ssd_prompt_kernel(xs_ref, z_ref, bm_ref, cm_ref, dt_ref, cw_ref, cb_ref, dtb_ref, alog_ref,
                       dsk_ref, gn_ref, e_ref, tri_ref, y_ref, st_out_ref,
                       xpad_ref, act_ref, st_ref):
    c = pl.program_id(1)
    L = SSD_CHUNK

    @pl.when(c == 0)
    def _():
        xpad_ref[0:SUBLANES, :] = jnp.zeros((SUBLANES, CONV_CH), F32)
        st_ref[...] = jnp.zeros(st_ref.shape, F32)

    xpad_ref[SUBLANES:SUBLANES + L, 0:D_INNER] = xs_ref[...]
    xpad_ref[SUBLANES:SUBLANES + L, D_INNER:D_INNER + 1024] = bm_ref[...]
    xpad_ref[SUBLANES:SUBLANES + L, D_INNER + 1024:CONV_CH] = cm_ref[...]

    slab = 512
    for s in range(CONV_CH // slab):
        cols = slice(s * slab, (s + 1) * slab)
        acc = cb_ref[:, cols] + cw_ref[0:1, cols] * xpad_ref[SUBLANES - 3:SUBLANES - 3 + L, cols]
        for w in range(1, CONV_W):
            acc = acc + cw_ref[w:w + 1, cols] * xpad_ref[SUBLANES - 3 + w:SUBLANES - 3 + w + L, cols]
        act_ref[:, cols] = _silu(acc)
    xpad_ref[0:SUBLANES, :] = xpad_ref[L:L + SUBLANES, :]

    dt = _softplus(dt_ref[...] + dtb_ref[...])
    a = -jnp.exp(alog_ref[...])
    acum = _dot_sel_lhs(tri_ref[...], dt * a)
    acum_t = acum.T
    dt_t = dt.T
    a_last = acum[L - 1:L, :]
    w_end = dt * jnp.exp(a_last - acum)
    ea = jnp.exp(acum)
    cd = jnp.broadcast_to(jnp.exp(a_last), (SUBLANES, LANES))
    e_sel = e_ref[...]
    w_exp = _dot_sel_rhs(w_end, e_sel)
    ea_exp = _dot_sel_rhs(ea, e_sel)
    cd_exp = _dot_sel_rhs(cd, e_sel)[0:1, :]

    row = lax.broadcasted_iota(jnp.int32, (L, L), 0)
    col = lax.broadcasted_iota(jnp.int32, (L, L), 1)
    causal = col <= row
    low_half = lax.broadcasted_iota(jnp.int32, (L, LANES), 1) < SSD_HEAD_DIM

    for g in range(N_SSD_GROUPS):
        cols = slice(g * GROUP_CH, (g + 1) * GROUP_CH)
        x_g = act_ref[:, cols]
        b_g = act_ref[:, D_INNER + g * D_STATE:D_INNER + (g + 1) * D_STATE]
        c_g = act_ref[:, D_INNER + 1024 + g * D_STATE:D_INNER + 1024 + (g + 1) * D_STATE]
        xb = x_g.astype(BF16)
        bb = b_g.astype(BF16)
        cbf = c_g.astype(BF16)
        cb = _dot_nt(cbf, bb)
        st = st_ref[g]
        y_g = _dot(cbf, st.astype(BF16)) * ea_exp[:, cols]
        pieces = []
        for pair in range(HEADS_PER_SSD_GROUP // 2):
            xp = xb[:, pair * LANES:(pair + 1) * LANES]
            ys = []
            for k in range(2):
                h = g * HEADS_PER_SSD_GROUP + pair * 2 + k
                seg = acum[:, h:h + 1] - acum_t[h:h + 1, :]
                dec = jnp.exp(jnp.where(causal, seg, -jnp.inf))
                m = (cb * dec * dt_t[h:h + 1, :]).astype(BF16)
                ys.append(_dot(m, xp))
            pieces.append(jnp.where(low_half, ys[0], ys[1]))
        y_g = y_g + jnp.concatenate(pieces, axis=-1) + dsk_ref[:, cols] * x_g
        xw = (x_g * w_exp[:, cols]).astype(BF16)
        st_ref[g] = st * cd_exp[:, cols] + _dot(b_g.T.astype(BF16), xw)
        yz = y_g * _silu(z_ref[:, cols])
        ms = jnp.mean(yz * yz, axis=-1, keepdims=True)
        y_ref[:, cols] = (yz * lax.rsqrt(ms + RMS_EPS) * gn_ref[:, cols]).astype(y_ref.dtype)

    @pl.when(c == pl.num_programs(1) - 1)
    def _():
        for g in range(N_SSD_GROUPS):
            st_out_ref[g * HEADS_PER_SSD_GROUP:(g + 1) * HEADS_PER_SSD_GROUP] = (
                st_ref[g].T.reshape(HEADS_PER_SSD_GROUP, SSD_HEAD_DIM, D_STATE))


def _ssd_prompt(zx, cw, cb, dtb, alog, dsk_exp, gn, e_sel, tri):
    b, s, _ = zx.shape
    nc = s // SSD_CHUNK
    L = SSD_CHUNK
    const = lambda bi, c: (0, 0)
    return pl.pallas_call(
        _ssd_prompt_kernel,
        grid=(b, nc),
        in_specs=[
            pl.BlockSpec((None, L, D_INNER), lambda bi, c: (bi, c, COL_XS // D_INNER)),
            pl.BlockSpec((None, L, D_INNER), lambda bi, c: (bi, c, COL_Z // D_INNER)),
            pl.BlockSpec((None, L, 1024), lambda bi, c: (bi, c, COL_B // 1024)),
            pl.BlockSpec((None, L, 1024), lambda bi, c: (bi, c, COL_C // 1024)),
            pl.BlockSpec((None, L, LANES), lambda bi, c: (bi, c, COL_DT // LANES)),
            pl.BlockSpec((CONV_W, CONV_CH), const),
            pl.BlockSpec((1, CONV_CH), const),
            pl.BlockSpec((1, LANES), const),
            pl.BlockSpec((1, LANES), const),
            pl.BlockSpec((1, D_INNER), const),
            pl.BlockSpec((1, D_INNER), const),
            pl.BlockSpec((LANES, D_INNER), const),
            pl.BlockSpec((L, L), const),
        ],
        out_specs=[
            pl.BlockSpec((None, L, D_INNER), lambda bi, c: (bi, c, 0)),
            pl.BlockSpec((None, N_SSD_HEADS, SSD_HEAD_DIM, D_STATE), lambda bi, c: (bi, 0, 0, 0)),
        ],
        out_shape=[
            jax.ShapeDtypeStruct((b, s, D_INNER), BF16),
            jax.ShapeDtypeStruct((b, N_SSD_HEADS, SSD_HEAD_DIM, D_STATE), F32),
        ],
        scratch_shapes=[
            pltpu.VMEM((L + 2 * SUBLANES, CONV_CH), F32),
            pltpu.VMEM((L, CONV_CH), F32),
            pltpu.VMEM((N_SSD_GROUPS, D_STATE, GROUP_CH), F32),
        ],
        compiler_params=_params("parallel", "arbitrary"),
        name="ssd_prompt",
    )(zx, zx, zx, zx, zx, cw, cb, dtb, alog, dsk_exp, gn, e_sel, tri)


def _band_softmax(s, v_bf16):
    m = jnp.max(s, axis=-1, keepdims=True)
    p = jnp.exp(s - m)
    l = jnp.sum(p, axis=-1, keepdims=True)
    return _dot(p.astype(BF16), v_bf16) / l, m + jnp.log(l)


def _attn_dense_kernel(q_ref, kc_ref, vc_ref, kp_ref, vp_ref, bias_ref, o_ref, l_ref):
    n = pl.program_id(1)
    kj = lax.broadcasted_iota(jnp.int32, (BAND, 2 * BAND), 1)
    before_start = kj < jnp.where(n == 0, BAND, 0)
    for h in range(HPG):
        sl = slice(h * ATTN_HEAD_DIM, (h + 1) * ATTN_HEAD_DIM)
        qh = (q_ref[:, sl] * ATTN_SCALE).astype(BF16)
        kh = jnp.concatenate([kp_ref[:, sl], kc_ref[:, sl]], axis=0).astype(BF16)
        vh = jnp.concatenate([vp_ref[:, sl], vc_ref[:, sl]], axis=0).astype(BF16)
        s = jnp.where(before_start, MASKED, _dot_nt(qh, kh) + bias_ref[h])
        o, lse = _band_softmax(s, vh)
        o_ref[:, sl] = o
        l_ref[:, sl] = jnp.broadcast_to(lse, (BAND, ATTN_HEAD_DIM))


def _attn_dilated_kernel(dil, q_ref, kc_ref, vc_ref, kp_ref, vp_ref, bias_ref, o_ref, l_ref):
    n = pl.program_id(1)
    kj = lax.broadcasted_iota(jnp.int32, (BAND, 2 * BAND), 1)
    before_start = kj < jnp.where(n == 0, BAND, 0)
    low = lax.broadcasted_iota(jnp.int32, (1, LANES), 1) < ATTN_HEAD_DIM
    for r in range(dil):
        rows = pl.ds(r, BAND, stride=dil)
        q = q_ref[rows, :] * ATTN_SCALE
        kb = jnp.concatenate([kp_ref[rows, :], kc_ref[rows, :]], axis=0).astype(BF16)
        vb = jnp.concatenate([vp_ref[rows, :], vc_ref[rows, :]], axis=0).astype(BF16)
        outs = []
        for k in range(2):
            qk = jnp.where(low if k == 0 else jnp.logical_not(low), q, 0.0).astype(BF16)
            s = jnp.where(before_start, MASKED, _dot_nt(qk, kb) + bias_ref[k])
            outs.append(_band_softmax(s, vb))
        o_ref[rows, :] = jnp.where(low, outs[0][0], outs[1][0])
        l_ref[rows, :] = jnp.where(low, outs[0][1], outs[1][1])


def _attn_prompt(zx, bias_qk, g):
    b, s, _ = zx.shape
    _, dil = DILATION_GROUPS[g]
    span = BAND * dil
    nb = s // span
    out_shape = [jax.ShapeDtypeStruct((b, s, ATTN_OUT), F32)] * 2
    if dil == 1:
        qb, kb, vb = COL_Q // ATTN_OUT + g, COL_K // ATTN_OUT + g, COL_V // ATTN_OUT + g
        blk = (None, BAND, ATTN_OUT)
        cur = lambda off: (lambda bi, n: (bi, n, off))
        prev = lambda off: (lambda bi, n: (bi, jnp.maximum(n - 1, 0), off))
        out_blk = pl.BlockSpec(blk, lambda bi, n: (bi, n, 0))
        return pl.pallas_call(
            _attn_dense_kernel,
            grid=(b, nb),
            in_specs=[pl.BlockSpec(blk, cur(qb)), pl.BlockSpec(blk, cur(kb)), pl.BlockSpec(blk, cur(vb)),
                      pl.BlockSpec(blk, prev(kb)), pl.BlockSpec(blk, prev(vb)),
                      pl.BlockSpec((HPG, BAND, 2 * BAND), lambda bi, n: (0, 0, 0))],
            out_specs=[out_blk, out_blk],
            out_shape=out_shape,
            compiler_params=_params("parallel", "arbitrary"),
            name=f"attn_prompt_g{g}",
        )(zx, zx, zx, zx, zx, bias_qk)
    pairs = HPG // 2
    qb, kb, vb = [(c + g * ATTN_OUT) // LANES for c in (COL_Q, COL_K, COL_V)]
    blk = (None, span, LANES)
    cur = lambda off: (lambda bi, n, hp: (bi, n, off + hp))
    prev = lambda off: (lambda bi, n, hp: (bi, jnp.maximum(n - 1, 0), off + hp))
    out_blk = pl.BlockSpec(blk, lambda bi, n, hp: (bi, n, hp))
    return pl.pallas_call(
        functools.partial(_attn_dilated_kernel, dil),
        grid=(b, nb, pairs),
        in_specs=[pl.BlockSpec(blk, cur(qb)), pl.BlockSpec(blk, cur(kb)), pl.BlockSpec(blk, cur(vb)),
                  pl.BlockSpec(blk, prev(kb)), pl.BlockSpec(blk, prev(vb)),
                  pl.BlockSpec((2, BAND, 2 * BAND), lambda bi, n, hp: (hp, 0, 0))],
        out_specs=[out_blk, out_blk],
        out_shape=out_shape,
        compiler_params=_params("parallel", "arbitrary", "arbitrary"),
        name=f"attn_prompt_g{g}",
    )(zx, zx, zx, zx, zx, bias_qk)


def _outproj_kernel(n_groups, *refs):
    x_ref, gate_ref, y_ref = refs[0:3]
    o_refs = refs[3:3 + n_groups]
    l_refs = refs[3 + n_groups:3 + 2 * n_groups] if n_groups > 1 else ()
    rest = refs[3 + (2 * n_groups if n_groups > 1 else 1):]
    ga_ref, ms0_ref, ms1_ref, ma0_ref, ma1_ref, wbs_ref, wba_ref, wo_ref, gp_ref, out_ref = rest

    if n_groups > 1:
        lses = [r[...] for r in l_refs]
        top = functools.reduce(jnp.maximum, lses)
        es = [jnp.exp(l - top) for l in lses]
        num = functools.reduce(lambda u, v: u + v, [e * r[...] for e, r in zip(es, o_refs)])
        o = num / functools.reduce(lambda u, v: u + v, es)
    else:
        o = o_refs[0][...]
    o = o * _silu(ga_ref[...])
    br_s = _dot(y_ref[...].astype(BF16), wbs_ref[...])
    br_a = _dot(o.astype(BF16), wba_ref[...])
    m_ssd = jnp.concatenate([ms0_ref[...], ms1_ref[...]], axis=-1)
    m_attn = jnp.concatenate([ma0_ref[...], ma1_ref[...]], axis=-1)
    merged = jax.nn.sigmoid(m_ssd) * br_s + jax.nn.sigmoid(m_attn) * br_a
    out = _dot(merged.astype(BF16), wo_ref[...])
    ms = jnp.mean(out * out, axis=-1, keepdims=True)
    normed = out * lax.rsqrt(ms + RMS_EPS) * gp_ref[...]
    out_ref[...] = x_ref[...] + gate_ref[...] * normed


def _outproj(x, mod, y, os_, ls_, zx, wbs, wba, wo, g_post, tm):
    b, t, _ = x.shape
    r = mod.shape[1]
    rb = 1 if r == 1 else tm
    n_groups = len(os_)
    row = lambda cb_: (lambda bi, i: (bi, i, cb_))
    mod_map = (lambda bi, i: (bi, 0, 2)) if r == 1 else (lambda bi, i: (bi, i, 2))
    const = lambda bi, i: (0, 0)
    in_specs = [
        pl.BlockSpec((None, tm, D_MODEL), row(0)),
        pl.BlockSpec((None, rb, D_MODEL), mod_map),
        pl.BlockSpec((None, tm, D_INNER), row(0)),
    ]
    in_specs += [pl.BlockSpec((None, tm, ATTN_OUT), row(0)) for _ in range(n_groups + len(ls_))]
    in_specs += [
        pl.BlockSpec((None, tm, ATTN_OUT), row(COL_GA // ATTN_OUT)),
        pl.BlockSpec((None, tm, 512), row(COL_MS // 512)),
        pl.BlockSpec((None, tm, 512), row(COL_MS // 512 + 1)),
        pl.BlockSpec((None, tm, 512), row(COL_MA // 512)),
        pl.BlockSpec((None, tm, 512), row(COL_MA // 512 + 1)),
        pl.BlockSpec((D_INNER, D_MODEL), const),
        pl.BlockSpec((ATTN_OUT, D_MODEL), const),
        pl.BlockSpec((D_MODEL, D_MODEL), const),
        pl.BlockSpec((1, D_MODEL), const),
    ]
    return pl.pallas_call(
        functools.partial(_outproj_kernel, n_groups),
        grid=(b, t // tm),
        in_specs=in_specs,
        out_specs=pl.BlockSpec((None, tm, D_MODEL), row(0)),
        out_shape=jax.ShapeDtypeStruct((b, t, D_MODEL), F32),
        compiler_params=_params("parallel", "parallel"),
        name="outproj",
    )(x, mod, y, *os_, *ls_, zx, zx, zx, zx, zx, wbs, wba, wo, g_post.reshape(1, D_MODEL))


def _ssd_pre_kernel(xs_ref, bm_ref, cm_ref, dt_ref, cs_ref, cw_ref, cb_ref, dtb_ref, alog_ref, e_ref,
                    conv_ref, act_ref, xdt_t_ref, cd_ref):
    parts = ((xs_ref, 0, D_INNER), (bm_ref, D_INNER, 1024), (cm_ref, D_INNER + 1024, 1024))
    for ref, off, width in parts:
        cols = slice(off, off + width)
        raw = ref[...]
        acc = cb_ref[:, cols] + cw_ref[CONV_W - 1:CONV_W, cols] * raw
        for w in range(CONV_W - 1):
            acc = acc + cw_ref[w:w + 1, cols] * cs_ref[:, w * CONV_CH + off:w * CONV_CH + off + width]
        act_ref[:, cols] = _silu(acc)
        for w in range(1, CONV_W - 1):
            conv_ref[:, (w - 1) * CONV_CH + off:(w - 1) * CONV_CH + off + width] = (
                cs_ref[:, w * CONV_CH + off:w * CONV_CH + off + width])
        conv_ref[:, (CONV_W - 2) * CONV_CH + off:(CONV_W - 2) * CONV_CH + off + width] = raw
    dt = _softplus(dt_ref[...] + dtb_ref[...])
    a = -jnp.exp(alog_ref[...])
    cd_ref[...] = jnp.exp(dt * a)
    xdt = act_ref[:, 0:D_INNER] * _dot_sel_rhs(dt, e_ref[...])
    xdt_t_ref[...] = xdt.T


def _ssd_pre(zs, conv_state, cw, cb, dtb, alog, e_sel):
    bd = zs.shape[0]
    const = lambda i: (0, 0)
    return pl.pallas_call(
        _ssd_pre_kernel,
        grid=(1,),
        in_specs=[
            pl.BlockSpec((bd, D_INNER), lambda i: (0, COL_XS // D_INNER)),
            pl.BlockSpec((bd, 1024), lambda i: (0, COL_B // 1024)),
            pl.BlockSpec((bd, 1024), lambda i: (0, COL_C // 1024)),
            pl.BlockSpec((bd, LANES), lambda i: (0, COL_DT // LANES)),
            pl.BlockSpec((bd, (CONV_W - 1) * CONV_CH), const),
            pl.BlockSpec((CONV_W, CONV_CH), const),
            pl.BlockSpec((1, CONV_CH), const),
            pl.BlockSpec((1, LANES), const),
            pl.BlockSpec((1, LANES), const),
            pl.BlockSpec((LANES, D_INNER), const),
        ],
        out_specs=[
            pl.BlockSpec((bd, (CONV_W - 1) * CONV_CH), const),
            pl.BlockSpec((bd, CONV_CH), const),
            pl.BlockSpec((D_INNER, bd), const),
            pl.BlockSpec((bd, LANES), const),
        ],
        out_shape=[
            jax.ShapeDtypeStruct((bd, (CONV_W - 1) * CONV_CH), F32),
            jax.ShapeDtypeStruct((bd, CONV_CH), F32),
            jax.ShapeDtypeStruct((D_INNER, bd), F32),
            jax.ShapeDtypeStruct((bd, LANES), F32),
        ],
        compiler_params=_params("arbitrary"),
        name="ssd_pre",
    )(zs, zs, zs, zs, conv_state, cw, cb, dtb, alog, e_sel)


SAMPLE_TILE = 8


def _ssd_state_kernel(cd_ref, st_ref, xt_ref, bm_ref, cm_ref, xs_ref, z_ref, dsk_ref, gn_ref,
                      so_ref, y_ref, yt_ref):
    t = pl.program_id(0)
    yt_ref[...] = jnp.zeros(yt_ref.shape, F32)

    def head_body(h, carry):
        g = h // HEADS_PER_SSD_GROUP
        lane0 = pl.multiple_of(g * D_STATE, D_STATE)
        row0 = pl.multiple_of(h * SSD_HEAD_DIM, SSD_HEAD_DIM)
        for j in range(SAMPLE_TILE):
            cd = cd_ref[(t * SAMPLE_TILE + j) * N_SSD_HEADS + h]
            xcol = xt_ref[pl.ds(row0, SSD_HEAD_DIM), j:j + 1]
            brow = bm_ref[j:j + 1, pl.ds(lane0, D_STATE)]
            crow = cm_ref[j:j + 1, pl.ds(lane0, D_STATE)]
            new = st_ref[j, h] * cd + xcol * brow
            so_ref[j, h] = new
            yt_ref[pl.ds(row0, SSD_HEAD_DIM), j:j + 1] = jnp.sum(new * crow, axis=-1, keepdims=True)
        return carry

    lax.fori_loop(0, N_SSD_HEADS, head_body, 0)

    y = yt_ref[...].T[0:SAMPLE_TILE, :] + dsk_ref[...] * xs_ref[...]
    yz = y * _silu(z_ref[...])
    for g in range(N_SSD_GROUPS):
        cols = slice(g * GROUP_CH, (g + 1) * GROUP_CH)
        v = yz[:, cols]
        ms = jnp.mean(v * v, axis=-1, keepdims=True)
        y_ref[:, cols] = v * lax.rsqrt(ms + RMS_EPS) * gn_ref[:, cols]


def _ssd_state(cd_flat, state, xdt_tiles, act, zs, dsk_exp, gn):
    bd = state.shape[0]
    nt = bd // SAMPLE_TILE
    const = lambda t: (0, 0)
    return pl.pallas_call(
        _ssd_state_kernel,
        grid=(nt,),
        in_specs=[
            pl.BlockSpec(memory_space=pltpu.SMEM),
            pl.BlockSpec((SAMPLE_TILE, N_SSD_HEADS, SSD_HEAD_DIM, D_STATE), lambda t: (t, 0, 0, 0)),
            pl.BlockSpec((None, D_INNER, SAMPLE_TILE), lambda t: (t, 0, 0)),
            pl.BlockSpec((SAMPLE_TILE, 1024), lambda t: (t, D_INNER // 1024)),
            pl.BlockSpec((SAMPLE_TILE, 1024), lambda t: (t, D_INNER // 1024 + 1)),
            pl.BlockSpec((SAMPLE_TILE, D_INNER), lambda t: (t, 0)),
            pl.BlockSpec((SAMPLE_TILE, D_INNER), lambda t: (t, COL_Z // D_INNER)),
            pl.BlockSpec((1, D_INNER), const),
            pl.BlockSpec((1, D_INNER), const),
        ],
        out_specs=[
            pl.BlockSpec((SAMPLE_TILE, N_SSD_HEADS, SSD_HEAD_DIM, D_STATE), lambda t: (t, 0, 0, 0)),
            pl.BlockSpec((SAMPLE_TILE, D_INNER), lambda t: (t, 0)),
        ],
        out_shape=[
            jax.ShapeDtypeStruct(state.shape, F32),
            jax.ShapeDtypeStruct((bd, D_INNER), F32),
        ],
        scratch_shapes=[pltpu.VMEM((D_INNER, LANES), F32)],
        compiler_params=_params("arbitrary"),
        name="ssd_state",
    )(cd_flat, state, xdt_tiles, act, act, act, zs, dsk_exp, gn)


def _attn_sample_kernel(qkv_ref, c0_ref, c1_ref, c2_ref, bc_ref, b0_ref, o_ref):
    outs, lses = [], []
    for g, c_ref in enumerate((c0_ref, c1_ref, c2_ref)):
        q = qkv_ref[g] * ATTN_SCALE
        kn = qkv_ref[N_DIL + g]
        vn = qkv_ref[2 * N_DIL + g]
        k = c_ref[:, 0]
        v = c_ref[:, 1]
        s = jnp.sum(k * q[None], axis=-1, keepdims=True) + bc_ref[g]
        sn = jnp.sum(kn * q, axis=-1, keepdims=True) + b0_ref[g]
        m = jnp.maximum(jnp.max(s, axis=0), sn)
        p = jnp.exp(s - m[None])
        pn = jnp.exp(sn - m)
        l = jnp.sum(p, axis=0) + pn
        outs.append((jnp.sum(p * v, axis=0) + pn * vn) / l)
        lses.append(m + jnp.log(l))
    top = functools.reduce(jnp.maximum, lses)
    es = [jnp.exp(l - top) for l in lses]
    den = functools.reduce(lambda u, v: u + v, es)
    o_ref[...] = functools.reduce(lambda u, v: u + v, [e * og for e, og in zip(es, outs)]) / den


def _attn_sample(qkv, caches, layer, bias_c, bias_0):
    bd = qkv.shape[0]
    views, specs = [], []
    for (window, dil), cache in zip(DILATION_GROUPS, caches):
        depth, _, wc = cache.shape[0:3]
        views.append(cache.reshape(depth, bd, wc // dil, dil, 2, HPG, ATTN_HEAD_DIM))
        specs.append(pl.BlockSpec((None, None, BAND, None, 2, HPG, ATTN_HEAD_DIM),
                                  lambda b: (layer, b, 0, 0, 0, 0, 0)))
    return pl.pallas_call(
        _attn_sample_kernel,
        grid=(bd,),
        in_specs=[
            pl.BlockSpec((None, 3 * N_DIL, HPG, ATTN_HEAD_DIM), lambda b: (b, 0, 0, 0)),
            *specs,
            pl.BlockSpec((N_DIL, BAND, HPG, ATTN_HEAD_DIM), lambda b: (0, 0, 0, 0)),
            pl.BlockSpec((N_DIL, HPG, ATTN_HEAD_DIM), lambda b: (0, 0, 0)),
        ],
        out_specs=pl.BlockSpec((None, HPG, ATTN_HEAD_DIM), lambda b: (b, 0, 0)),
        out_shape=jax.ShapeDtypeStruct((bd, HPG, ATTN_HEAD_DIM), F32),
        compiler_params=_params("parallel"),
        name="attn_sample",
    )(qkv, *views, bias_c, bias_0)


def _t5_bucket(dist):
    max_exact = N_BUCKETS // 2
    n = jnp.maximum(dist, 1).astype(F32)
    large = max_exact + (jnp.log(n / max_exact) / math.log(MAX_DISTANCE / max_exact)
                         * (N_BUCKETS - max_exact)).astype(jnp.int32)
    large = jnp.minimum(large, N_BUCKETS - 1)
    return jnp.where(dist < max_exact, dist, large)


def _group_bias(rel_bias, g):
    _, dil = DILATION_GROUPS[g]
    dist = jnp.arange(BAND + 1, dtype=jnp.int32) * dil
    return rel_bias[_t5_bucket(dist), g * HPG:(g + 1) * HPG].astype(F32)


def _prompt_bias(bias):
    qi = np.arange(BAND)[:, None]
    kj = np.arange(2 * BAND)[None, :]
    diff = qi + BAND - kj
    in_band = (diff >= 0) & (diff <= BAND)
    table = jnp.transpose(bias[np.clip(diff, 0, BAND)], (2, 0, 1))
    return jnp.where(jnp.asarray(in_band)[None], table, MASKED)


def _reorder_w_in(w):
    z, xs, bm, cm, dt, q, k, v, ga, ms, ma = jnp.split(
        w, np.cumsum([2048, 2048, 1024, 1024, 32, 2304, 2304, 2304, 768, 1024])[:].tolist(), axis=-1)
    pad = jnp.zeros((w.shape[0], N_COLS - COL_DT - N_SSD_HEADS), w.dtype)
    return jnp.concatenate([xs, z, bm, cm, q, k, v, ga, ms, ma, dt, pad], axis=-1).astype(BF16)


def _lane_pad(v):
    return jnp.pad(v.astype(F32), (0, LANES - v.shape[0])).reshape(1, LANES)


def kernel(x_prompt, x_sample, cache_kv_w128, cache_kv_w512, cache_kv_w2048, state_conv, state_ssm,
           c_prompt, c_sample, rel_bias, w_ada, b_ada, g_pre, w_in, conv_w, conv_b, dt_bias, a_log,
           d_skip, g_ssd_norm, w_br_ssd, w_br_attn, w_out, g_post):
    depth = w_in.shape[0]
    bp, seq, _ = x_prompt.shape
    bd = x_sample.shape[0]
    assert x_sample.shape[1] == 1 and seq % (BAND * 16) == 0 and bd % SAMPLE_TILE == 0 and bd % LANES == 0
    caches = (cache_kv_w128, cache_kv_w512, cache_kv_w2048)

    head_of_ch = np.arange(D_INNER) // SSD_HEAD_DIM
    e_sel = jnp.asarray(head_of_ch[None, :] == np.arange(LANES)[:, None], BF16)
    tri = jnp.asarray(np.tril(np.ones((SSD_CHUNK, SSD_CHUNK))), BF16)

    gb = [_group_bias(rel_bias, g) for g in range(N_DIL)]
    bias_qk = [_prompt_bias(b) for b in gb]
    over_ch = lambda u: jnp.broadcast_to(u[..., None], u.shape + (ATTN_HEAD_DIM,))
    bias_c = jnp.stack([over_ch(b[BAND:0:-1]) for b in gb])
    bias_0 = jnp.stack([over_ch(b[0]) for b in gb])

    n_c = bp + bd
    rows = -(-n_c // SUBLANES) * SUBLANES
    c_all = jnp.pad(jnp.concatenate([c_prompt, c_sample], axis=0), ((0, rows - n_c), (0, 0)))
    mod_all = _ada(c_all, w_ada, b_ada)

    xp = x_prompt
    xs = x_sample.reshape(1, bd, D_MODEL)
    outs = {k: [] for k in ("kvp0", "kvp1", "kvp2", "kvs0", "kvs1", "kvs2", "convp", "convs", "ssmp", "ssms")}
    for l in range(depth):
        w_l = _reorder_w_in(w_in[l])
        wbs, wba, wo = w_br_ssd[l].astype(BF16), w_br_attn[l].astype(BF16), w_out[l].astype(BF16)
        cw, cb = conv_w[l], conv_b[l].reshape(1, CONV_CH)
        dtb, alog = _lane_pad(dt_bias[l]), _lane_pad(a_log[l])
        dsk_exp = jnp.repeat(d_skip[l].astype(F32), SSD_HEAD_DIM).reshape(1, D_INNER)
        gn = g_ssd_norm[l].reshape(1, D_INNER)
        mod_p = mod_all[l, 0:bp].reshape(bp, 1, 3 * D_MODEL)
        mod_s = mod_all[l, bp:n_c].reshape(1, bd, 3 * D_MODEL)

        zx = _inproj(xp, mod_p, g_pre[l], w_l, tm=2048, tn=ATTN_OUT)
        y_p, ssm_p = _ssd_prompt(zx, cw, cb, dtb, alog, dsk_exp, gn, e_sel, tri)
        attn = [_attn_prompt(zx, bias_qk[g], g) for g in range(N_DIL)]
        xp = _outproj(xp, mod_p, y_p, [a[0] for a in attn], [a[1] for a in attn], zx, wbs, wba, wo,
                      g_post[l], tm=512)
        for g, (window, _) in enumerate(DILATION_GROUPS):
            kk = zx[:, seq - window:, COL_K + g * ATTN_OUT:COL_K + (g + 1) * ATTN_OUT]
            vv = zx[:, seq - window:, COL_V + g * ATTN_OUT:COL_V + (g + 1) * ATTN_OUT]
            outs[f"kvp{g}"].append(jnp.stack([kk, vv], axis=2).reshape(bp, window, 2, HPG, ATTN_HEAD_DIM))
        outs["convp"].append(jnp.concatenate(
            [zx[:, seq - (CONV_W - 1):, COL_XS:COL_XS + D_INNER], zx[:, seq - (CONV_W - 1):, COL_B:COL_B + 2048]],
            axis=-1))
        outs["ssmp"].append(ssm_p)

        zs = _inproj(xs, mod_s, g_pre[l], w_l, tm=bd, tn=ATTN_OUT)[0]
        new_conv, act, xdt_t, cd = _ssd_pre(zs, state_conv[l].reshape(bd, (CONV_W - 1) * CONV_CH),
                                            cw, cb, dtb, alog, e_sel)
        xdt_tiles = xdt_t.reshape(D_INNER, bd // SAMPLE_TILE, SAMPLE_TILE).transpose(1, 0, 2)
        new_ssm, y_s = _ssd_state(cd[:, 0:N_SSD_HEADS].reshape(-1), state_ssm[l], xdt_tiles, act, zs, dsk_exp, gn)
        qkv = zs[:, COL_Q:COL_Q + 3 * ATTN_QKV].reshape(bd, 3 * N_DIL, HPG, ATTN_HEAD_DIM)
        o_s = _attn_sample(qkv, caches, l, bias_c, bias_0)
        xs = _outproj(xs, mod_s, y_s.reshape(1, bd, D_INNER), [o_s.reshape(1, bd, ATTN_OUT)], [],
                      zs.reshape(1, bd, N_COLS), wbs, wba, wo, g_post[l], tm=bd)
        for g in range(N_DIL):
            kk = zs[:, COL_K + g * ATTN_OUT:COL_K + (g + 1) * ATTN_OUT]
            vv = zs[:, COL_V + g * ATTN_OUT:COL_V + (g + 1) * ATTN_OUT]
            outs[f"kvs{g}"].append(jnp.stack([kk, vv], axis=1).reshape(bd, 1, 2, HPG, ATTN_HEAD_DIM))
        outs["convs"].append(new_conv.reshape(bd, CONV_W - 1, CONV_CH))
        outs["ssms"].append(new_ssm)

    st = lambda key: jnp.stack(outs[key], axis=0)
    return (xp, xs.reshape(bd, 1, D_MODEL), st("kvp0"), st("kvs0"), st("kvp1"), st("kvs1"), st("kvp2"), st("kvs2"),
            st("convp"), st("convs"), st("ssmp"), st("ssms"))
```

```python
import functools
import math

import numpy as np
import jax
import jax.numpy as jnp
from jax import lax
from jax.experimental import pallas as pl
from jax.experimental.pallas import tpu as pltpu

F32 = jnp.float32
BF16 = jnp.bfloat16

D_MODEL = 1024
D_INNER = 2048
SSD_HEAD_DIM = 64
N_SSD_HEADS = 32
D_STATE = 128
N_SSD_GROUPS = 8
HEADS_PER_SSD_GROUP = 4
GROUP_CH = D_INNER // N_SSD_GROUPS
CONV_W = 4
CONV_CH = 4096
SSD_CHUNK = 128
DILATION_GROUPS = ((128, 1), (512, 4), (2048, 16))
N_DIL = 3
ATTN_HEAD_DIM = 64
HPG = 12
ATTN_QKV = 2304
ATTN_OUT = 768
ATTN_SCALE = ATTN_HEAD_DIM ** -0.5
N_BUCKETS = 32
MAX_DISTANCE = 2048
RMS_EPS = 1e-6
BAND = 128
MASKED = -1e30

LANES = 128
SUBLANES = 8
VMEM_LIMIT = 56 * 1024 * 1024

COL_XS = 0
COL_Z = 2048
COL_B = 4096
COL_C = 5120
COL_Q = 6144
COL_K = 8448
COL_V = 10752
COL_GA = 13056
COL_MS = 13824
COL_MA = 14848
COL_DT = 15872
N_COLS = 16128
QKV_BLOCKS = N_COLS // ATTN_OUT


def _dot(a, b):
    return jnp.dot(a, b, preferred_element_type=F32)


def _dot_nt(a, b):
    return lax.dot_general(a, b, (((1,), (1,)), ((), ())), preferred_element_type=F32)


def _split3(x):
    x1 = x.astype(BF16)
    r1 = x - x1.astype(F32)
    x2 = r1.astype(BF16)
    x3 = (r1 - x2.astype(F32)).astype(BF16)
    return x1, x2, x3


def _dot_sel_rhs(x, sel):
    x1, x2, x3 = _split3(x)
    return _dot(x1, sel) + _dot(x2, sel) + _dot(x3, sel)


def _dot_sel_lhs(sel, x):
    x1, x2, x3 = _split3(x)
    return _dot(sel, x1) + _dot(sel, x2) + _dot(sel, x3)


def _sigmoid(x):
    return 0.5 + 0.5 * jnp.tanh(0.5 * x)


def _silu(x):
    t = 0.5 * x
    return t + t * jnp.tanh(t)


def _softplus(x):
    return jnp.maximum(x, 0.0) + jnp.log1p(jnp.exp(-jnp.abs(x)))


def _params(*sem):
    return pltpu.CompilerParams(dimension_semantics=sem, vmem_limit_bytes=VMEM_LIMIT)


def _ada_kernel(c_ref, w_ref, b_ref, o_ref):
    c = c_ref[...]
    o_ref[...] = _dot(_silu(c).astype(BF16), w_ref[...].astype(BF16)) + b_ref[...]


def _ada(c_all, w_ada, b_ada):
    depth = w_ada.shape[0]
    rows = c_all.shape[0]
    nj = 3 * D_MODEL // 1024
    return pl.pallas_call(
        _ada_kernel,
        grid=(depth, nj),
        in_specs=[
            pl.BlockSpec((rows, D_MODEL), lambda l, j: (0, 0)),
            pl.BlockSpec((None, D_MODEL, 1024), lambda l, j: (l, 0, j)),
            pl.BlockSpec((None, 1, 1024), lambda l, j: (l, 0, j)),
        ],
        out_specs=pl.BlockSpec((None, rows, 1024), lambda l, j: (l, 0, j)),
        out_shape=jax.ShapeDtypeStruct((depth, rows, 3 * D_MODEL), F32),
        compiler_params=_params("parallel", "parallel"),
        name="ada",
    )(c_all, w_ada, b_ada.reshape(depth, 1, 3 * D_MODEL))


def _inproj_kernel(x_ref, mod_ref, g_ref, w_ref, o_ref, h_ref):
    @pl.when(pl.program_id(2) == 0)
    def _():
        x = x_ref[...]
        ms = jnp.mean(x * x, axis=-1, keepdims=True)
        y = x * lax.rsqrt(ms + RMS_EPS) * g_ref[...]
        shift = mod_ref[:, 0:D_MODEL]
        scale = mod_ref[:, D_MODEL:2 * D_MODEL]
        h_ref[...] = (y * (1.0 + scale) + shift).astype(BF16)

    o_ref[...] = _dot(h_ref[...], w_ref[...])


def _inproj(x, mod, g_pre, w_bf16, tm, tn):
    b, t, _ = x.shape
    r = mod.shape[1]
    rb = 1 if r == 1 else tm
    mod_map = (lambda bi, i, j: (bi, 0, 0)) if r == 1 else (lambda bi, i, j: (bi, i, 0))
    return pl.pallas_call(
        _inproj_kernel,
        grid=(b, t // tm, N_COLS // tn),
        in_specs=[
            pl.BlockSpec((None, tm, D_MODEL), lambda bi, i, j: (bi, i, 0)),
            pl.BlockSpec((None, rb, 3 * D_MODEL), mod_map),
            pl.BlockSpec((1, D_MODEL), lambda bi, i, j: (0, 0)),
            pl.BlockSpec((D_MODEL, tn), lambda bi, i, j: (0, j)),
        ],
        out_specs=pl.BlockSpec((None, tm, tn), lambda bi, i, j: (bi, i, j)),
        out_shape=jax.ShapeDtypeStruct((b, t, N_COLS), F32),
        scratch_shapes=[pltpu.VMEM((tm, D_MODEL), BF16)],
        compiler_params=_params("parallel", "parallel", "arbitrary"),
        name="inproj",
    )(x, mod, g_pre.reshape(1, D_MODEL), w_bf16)


def _ssd_prompt_kernel(xs_ref, z_ref, bm_ref, cm_ref, dt_ref, cw_ref, cb_ref, dtb_ref, alog_ref,
                       dsk_ref, gn_ref, e_ref, tri_ref, y_ref, st_out_ref,
                       xpad_ref, act_ref, st_ref):
    c = pl.program_id(1)
    L = SSD_CHUNK

    @pl.when(c == 0)
    def _():
        xpad_ref[0:SUBLANES, :] = jnp.zeros((SUBLANES, CONV_CH), F32)
        st_ref[...] = jnp.zeros(st_ref.shape, F32)

    xpad_ref[SUBLANES:SUBLANES + L, 0:D_INNER] = xs_ref[...]
    xpad_ref[SUBLANES:SUBLANES + L, D_INNER:D_INNER + 1024] = bm_ref[...]
    xpad_ref[SUBLANES:SUBLANES + L, D_INNER + 1024:CONV_CH] = cm_ref[...]

    slab = 512
    for s in range(CONV_CH // slab):
        cols = slice(s * slab, (s + 1) * slab)
        acc = cb_ref[:, cols] + cw_ref[0:1, cols] * xpad_ref[SUBLANES - 3:SUBLANES - 3 + L, cols]
        for w in range(1, CONV_W):
            acc = acc + cw_ref[w:w + 1, cols] * xpad_ref[SUBLANES - 3 + w:SUBLANES - 3 + w + L, cols]
        act_ref[:, cols] = _silu(acc)
    xpad_ref[0:SUBLANES, :] = xpad_ref[L:L + SUBLANES, :]

    dt = _softplus(dt_ref[...] + dtb_ref[...])
    a = -jnp.exp(alog_ref[...])
    acum = _dot_sel_lhs(tri_ref[...], dt * a)
    acum_t = acum.T
    dt_t = dt.T
    a_last = acum[L - 1:L, :]
    w_end = dt * jnp.exp(a_last - acum)
    ea = jnp.exp(acum)
    cd = jnp.broadcast_to(jnp.exp(a_last), (SUBLANES, LANES))
    e_sel = e_ref[...]
    w_exp = _dot_sel_rhs(w_end, e_sel)
    ea_exp = _dot_sel_rhs(ea, e_sel)
    cd_exp = _dot_sel_rhs(cd, e_sel)[0:1, :]

    row = lax.broadcasted_iota(jnp.int32, (L, L), 0)
    col = lax.broadcasted_iota(jnp.int32, (L, L), 1)
    causal = col <= row
    low_half = lax.broadcasted_iota(jnp.int32, (L, LANES), 1) < SSD_HEAD_DIM

    for g in range(N_SSD_GROUPS):
        cols = slice(g * GROUP_CH, (g + 1) * GROUP_CH)
        x_g = act_ref[:, cols]
        b_g = act_ref[:, D_INNER + g * D_STATE:D_INNER + (g + 1) * D_STATE]
        c_g = act_ref[:, D_INNER + 1024 + g * D_STATE:D_INNER + 1024 + (g + 1) * D_STATE]
        xb = x_g.astype(BF16)
        bb = b_g.astype(BF16)
        cbf = c_g.astype(BF16)
        cb = _dot_nt(cbf, bb)
        st = st_ref[g]
        y_g = _dot(cbf, st.astype(BF16)) * ea_exp[:, cols]
        pieces = []
        for pair in range(HEADS_PER_SSD_GROUP // 2):
            xp = xb[:, pair * LANES:(pair + 1) * LANES]
            ys = []
            for k in range(2):
                h = g * HEADS_PER_SSD_GROUP + pair * 2 + k
                seg = acum[:, h:h + 1] - acum_t[h:h + 1, :]
                dec = jnp.exp(jnp.where(causal, seg, -jnp.inf))
                m = (cb * dec * dt_t[h:h + 1, :]).astype(BF16)
                ys.append(_dot(m, xp))
            pieces.append(jnp.where(low_half, ys[0], ys[1]))
        y_g = y_g + jnp.concatenate(pieces, axis=-1) + dsk_ref[:, cols] * x_g
        xw = (x_g * w_exp[:, cols]).astype(BF16)
        st_ref[g] = st * cd_exp[:, cols] + _dot(b_g.T.astype(BF16), xw)
        yz = y_g * _silu(z_ref[:, cols])
        ms = jnp.mean(yz * yz, axis=-1, keepdims=True)
        y_ref[:, cols] = (yz * lax.rsqrt(ms + RMS_EPS) * gn_ref[:, cols]).astype(y_ref.dtype)

    @pl.when(c == pl.num_programs(1) - 1)
    def _():
        for g in range(N_SSD_GROUPS):
            st_out_ref[g * HEADS_PER_SSD_GROUP:(g + 1) * HEADS_PER_SSD_GROUP] = (
                st_ref[g].T.reshape(HEADS_PER_SSD_GROUP, SSD_HEAD_DIM, D_STATE))


def _ssd_prompt(zx, cw, cb, dtb, alog, dsk_exp, gn, e_sel, tri):
    b, s, _ = zx.shape
    nc = s // SSD_CHUNK
    L = SSD_CHUNK
    const = lambda bi, c: (0, 0)
    return pl.pallas_call(
        _ssd_prompt_kernel,
        grid=(b, nc),
        in_specs=[
            pl.BlockSpec((None, L, D_INNER), lambda bi, c: (bi, c, COL_XS // D_INNER)),
            pl.BlockSpec((None, L, D_INNER), lambda bi, c: (bi, c, COL_Z // D_INNER)),
            pl.BlockSpec((None, L, 1024), lambda bi, c: (bi, c, COL_B // 1024)),
            pl.BlockSpec((None, L, 1024), lambda bi, c: (bi, c, COL_C // 1024)),
            pl.BlockSpec((None, L, LANES), lambda bi, c: (bi, c, COL_DT // LANES)),
            pl.BlockSpec((CONV_W, CONV_CH), const),
            pl.BlockSpec((1, CONV_CH), const),
            pl.BlockSpec((1, LANES), const),
            pl.BlockSpec((1, LANES), const),
            pl.BlockSpec((1, D_INNER), const),
            pl.BlockSpec((1, D_INNER), const),
            pl.BlockSpec((LANES, D_INNER), const),
            pl.BlockSpec((L, L), const),
        ],
        out_specs=[
            pl.BlockSpec((None, L, D_INNER), lambda bi, c: (bi, c, 0)),
            pl.BlockSpec((None, N_SSD_HEADS, SSD_HEAD_DIM, D_STATE), lambda bi, c: (bi, 0, 0, 0)),
        ],
        out_shape=[
            jax.ShapeDtypeStruct((b, s, D_INNER), BF16),
            jax.ShapeDtypeStruct((b, N_SSD_HEADS, SSD_HEAD_DIM, D_STATE), F32),
        ],
        scratch_shapes=[
            pltpu.VMEM((L + 2 * SUBLANES, CONV_CH), F32),
            pltpu.VMEM((L, CONV_CH), F32),
            pltpu.VMEM((N_SSD_GROUPS, D_STATE, GROUP_CH), F32),
        ],
        compiler_params=_params("parallel", "arbitrary"),
        name="ssd_prompt",
    )(zx, zx, zx, zx, zx, cw, cb, dtb, alog, dsk_exp, gn, e_sel, tri)


def _head_pair_attention(q, kb, vb, bias_ref, pair_bias0, before_start):
    low = lax.broadcasted_iota(jnp.int32, (1, LANES), 1) < ATTN_HEAD_DIM
    os_, ls_ = [], []
    for k in range(2):
        qk = jnp.where(low if k == 0 else jnp.logical_not(low), q, 0.0).astype(BF16)
        s = jnp.where(before_start, MASKED, _dot_nt(qk, kb) + bias_ref[pair_bias0 + k])
        m = jnp.max(s, axis=-1, keepdims=True)
        p = jnp.exp(s - m)
        l = jnp.sum(p, axis=-1, keepdims=True)
        os_.append(_dot(p.astype(BF16), vb) / l)
        ls_.append(m + jnp.log(l))
    return jnp.where(low, os_[0], os_[1]), jnp.where(low, ls_[0], ls_[1])


def _attn_dense_kernel(q_ref, kc_ref, vc_ref, kp_ref, vp_ref, bias_ref, o_ref, l_ref):
    n = pl.program_id(1)
    kj = lax.broadcasted_iota(jnp.int32, (BAND, 2 * BAND), 1)
    before_start = kj < jnp.where(n == 0, BAND, 0)
    for pair in range(HPG // 2):
        sl = slice(pair * LANES, (pair + 1) * LANES)
        kb = jnp.concatenate([kp_ref[:, sl], kc_ref[:, sl]], axis=0).astype(BF16)
        vb = jnp.concatenate([vp_ref[:, sl], vc_ref[:, sl]], axis=0).astype(BF16)
        o, lse = _head_pair_attention(q_ref[:, sl] * ATTN_SCALE, kb, vb, bias_ref, 2 * pair, before_start)
        o_ref[:, sl] = o
        l_ref[:, sl] = lse


def _attn_dilated_kernel(dil, q_ref, kc_ref, vc_ref, kp_ref, vp_ref, bias_ref, o_ref, l_ref):
    n = pl.program_id(1)
    kj = lax.broadcasted_iota(jnp.int32, (BAND, 2 * BAND), 1)
    before_start = kj < jnp.where(n == 0, BAND, 0)
    for r in range(dil):
        rows = pl.ds(r, BAND, stride=dil)
        kb = jnp.concatenate([kp_ref[rows, :], kc_ref[rows, :]], axis=0).astype(BF16)
        vb = jnp.concatenate([vp_ref[rows, :], vc_ref[rows, :]], axis=0).astype(BF16)
        o, lse = _head_pair_attention(q_ref[rows, :] * ATTN_SCALE, kb, vb, bias_ref, 0, before_start)
        o_ref[rows, :] = o
        l_ref[rows, :] = lse


def _attn_prompt(zx, bias_qk, g):
    b, s, _ = zx.shape
    _, dil = DILATION_GROUPS[g]
    span = BAND * dil
    nb = s // span
    out_shape = [jax.ShapeDtypeStruct((b, s, ATTN_OUT), F32)] * 2
    if dil == 1:
        qb, kb, vb = COL_Q // ATTN_OUT + g, COL_K // ATTN_OUT + g, COL_V // ATTN_OUT + g
        blk = (None, BAND, ATTN_OUT)
        cur = lambda off: (lambda bi, n: (bi, n, off))
        prev = lambda off: (lambda bi, n: (bi, jnp.maximum(n - 1, 0), off))
        out_blk = pl.BlockSpec(blk, lambda bi, n: (bi, n, 0))
        return pl.pallas_call(
            _attn_dense_kernel,
            grid=(b, nb),
            in_specs=[pl.BlockSpec(blk, cur(qb)), pl.BlockSpec(blk, cur(kb)), pl.BlockSpec(blk, cur(vb)),
                      pl.BlockSpec(blk, prev(kb)), pl.BlockSpec(blk, prev(vb)),
                      pl.BlockSpec((HPG, BAND, 2 * BAND), lambda bi, n: (0, 0, 0))],
            out_specs=[out_blk, out_blk],
            out_shape=out_shape,
            compiler_params=_params("parallel", "arbitrary"),
            name=f"attn_prompt_g{g}",
        )(zx, zx, zx, zx, zx, bias_qk)
    pairs = HPG // 2
    qb, kb, vb = [(c + g * ATTN_OUT) // LANES for c in (COL_Q, COL_K, COL_V)]
    blk = (None, span, LANES)
    cur = lambda off: (lambda bi, n, hp: (bi, n, off + hp))
    prev = lambda off: (lambda bi, n, hp: (bi, jnp.maximum(n - 1, 0), off + hp))
    out_blk = pl.BlockSpec(blk, lambda bi, n, hp: (bi, n, hp))
    return pl.pallas_call(
        functools.partial(_attn_dilated_kernel, dil),
        grid=(b, nb, pairs),
        in_specs=[pl.BlockSpec(blk, cur(qb)), pl.BlockSpec(blk, cur(kb)), pl.BlockSpec(blk, cur(vb)),
                  pl.BlockSpec(blk, prev(kb)), pl.BlockSpec(blk, prev(vb)),
                  pl.BlockSpec((2, BAND, 2 * BAND), lambda bi, n, hp: (hp, 0, 0))],
        out_specs=[out_blk, out_blk],
        out_shape=out_shape,
        compiler_params=_params("parallel", "arbitrary", "arbitrary"),
        name=f"attn_prompt_g{g}",
    )(zx, zx, zx, zx, zx, bias_qk)


def _outproj_kernel(n_groups, *refs):
    x_ref, gate_ref, y_ref = refs[0:3]
    o_refs = refs[3:3 + n_groups]
    l_refs = refs[3 + n_groups:3 + 2 * n_groups] if n_groups > 1 else ()
    rest = refs[3 + (2 * n_groups if n_groups > 1 else 1):]
    ga_ref, ms0_ref, ms1_ref, ma0_ref, ma1_ref, wbs_ref, wba_ref, wo_ref, gp_ref, out_ref = rest

    if n_groups > 1:
        lses = [r[...] for r in l_refs]
        top = functools.reduce(jnp.maximum, lses)
        es = [jnp.exp(l - top) for l in lses]
        num = functools.reduce(lambda u, v: u + v, [e * r[...] for e, r in zip(es, o_refs)])
        o = num / functools.reduce(lambda u, v: u + v, es)
    else:
        o = o_refs[0][...]
    o = o * _silu(ga_ref[...])
    br_s = _dot(y_ref[...].astype(BF16), wbs_ref[...])
    br_a = _dot(o.astype(BF16), wba_ref[...])
    m_ssd = jnp.concatenate([ms0_ref[...], ms1_ref[...]], axis=-1)
    m_attn = jnp.concatenate([ma0_ref[...], ma1_ref[...]], axis=-1)
    merged = _sigmoid(m_ssd) * br_s + _sigmoid(m_attn) * br_a
    out = _dot(merged.astype(BF16), wo_ref[...])
    ms = jnp.mean(out * out, axis=-1, keepdims=True)
    normed = out * lax.rsqrt(ms + RMS_EPS) * gp_ref[...]
    out_ref[...] = x_ref[...] + gate_ref[...] * normed


def _outproj(x, mod, y, os_, ls_, zx, wbs, wba, wo, g_post, tm):
    b, t, _ = x.shape
    r = mod.shape[1]
    rb = 1 if r == 1 else tm
    n_groups = len(os_)
    row = lambda cb_: (lambda bi, i: (bi, i, cb_))
    mod_map = (lambda bi, i: (bi, 0, 2)) if r == 1 else (lambda bi, i: (bi, i, 2))
    const = lambda bi, i: (0, 0)
    in_specs = [
        pl.BlockSpec((None, tm, D_MODEL), row(0)),
        pl.BlockSpec((None, rb, D_MODEL), mod_map),
        pl.BlockSpec((None, tm, D_INNER), row(0)),
    ]
    in_specs += [pl.BlockSpec((None, tm, ATTN_OUT), row(0)) for _ in range(n_groups + len(ls_))]
    in_specs += [
        pl.BlockSpec((None, tm, ATTN_OUT), row(COL_GA // ATTN_OUT)),
        pl.BlockSpec((None, tm, 512), row(COL_MS // 512)),
        pl.BlockSpec((None, tm, 512), row(COL_MS // 512 + 1)),
        pl.BlockSpec((None, tm, 512), row(COL_MA // 512)),
        pl.BlockSpec((None, tm, 512), row(COL_MA // 512 + 1)),
        pl.BlockSpec((D_INNER, D_MODEL), const),
        pl.BlockSpec((ATTN_OUT, D_MODEL), const),
        pl.BlockSpec((D_MODEL, D_MODEL), const),
        pl.BlockSpec((1, D_MODEL), const),
    ]
    return pl.pallas_call(
        functools.partial(_outproj_kernel, n_groups),
        grid=(b, t // tm),
        in_specs=in_specs,
        out_specs=pl.BlockSpec((None, tm, D_MODEL), row(0)),
        out_shape=jax.ShapeDtypeStruct((b, t, D_MODEL), F32),
        compiler_params=_params("parallel", "parallel"),
        name="outproj",
    )(x, mod, y, *os_, *ls_, zx, zx, zx, zx, zx, wbs, wba, wo, g_post.reshape(1, D_MODEL))


def _ssd_pre_kernel(xs_ref, bm_ref, cm_ref, dt_ref, cs_ref, cw_ref, cb_ref, dtb_ref, alog_ref, e_ref,
                    conv_ref, act_ref, xdt_t_ref, cd_ref):
    parts = ((xs_ref, 0, D_INNER), (bm_ref, D_INNER, 1024), (cm_ref, D_INNER + 1024, 1024))
    for ref, off, width in parts:
        cols = slice(off, off + width)
        raw = ref[...]
        acc = cb_ref[:, cols] + cw_ref[CONV_W - 1:CONV_W, cols] * raw
        for w in range(CONV_W - 1):
            acc = acc + cw_ref[w:w + 1, cols] * cs_ref[:, w * CONV_CH + off:w * CONV_CH + off + width]
        act_ref[:, cols] = _silu(acc)
        for w in range(1, CONV_W - 1):
            conv_ref[:, (w - 1) * CONV_CH + off:(w - 1) * CONV_CH + off + width] = (
                cs_ref[:, w * CONV_CH + off:w * CONV_CH + off + width])
        conv_ref[:, (CONV_W - 2) * CONV_CH + off:(CONV_W - 2) * CONV_CH + off + width] = raw
    dt = _softplus(dt_ref[...] + dtb_ref[...])
    a = -jnp.exp(alog_ref[...])
    cd_ref[...] = jnp.exp(dt * a)
    xdt = act_ref[:, 0:D_INNER] * _dot_sel_rhs(dt, e_ref[...])
    xdt_t_ref[...] = xdt.T


def _ssd_pre(zs, conv_state, cw, cb, dtb, alog, e_sel):
    bd = zs.shape[0]
    const = lambda i: (0, 0)
    return pl.pallas_call(
        _ssd_pre_kernel,
        grid=(1,),
        in_specs=[
            pl.BlockSpec((bd, D_INNER), lambda i: (0, COL_XS // D_INNER)),
            pl.BlockSpec((bd, 1024), lambda i: (0, COL_B // 1024)),
            pl.BlockSpec((bd, 1024), lambda i: (0, COL_C // 1024)),
            pl.BlockSpec((bd, LANES), lambda i: (0, COL_DT // LANES)),
            pl.BlockSpec((bd, (CONV_W - 1) * CONV_CH), const),
            pl.BlockSpec((CONV_W, CONV_CH), const),
            pl.BlockSpec((1, CONV_CH), const),
            pl.BlockSpec((1, LANES), const),
            pl.BlockSpec((1, LANES), const),
            pl.BlockSpec((LANES, D_INNER), const),
        ],
        out_specs=[
            pl.BlockSpec((bd, (CONV_W - 1) * CONV_CH), const),
            pl.BlockSpec((bd, CONV_CH), const),
            pl.BlockSpec((D_INNER, bd), const),
            pl.BlockSpec((bd, LANES), const),
        ],
        out_shape=[
            jax.ShapeDtypeStruct((bd, (CONV_W - 1) * CONV_CH), F32),
            jax.ShapeDtypeStruct((bd, CONV_CH), F32),
            jax.ShapeDtypeStruct((D_INNER, bd), F32),
            jax.ShapeDtypeStruct((bd, LANES), F32),
        ],
        compiler_params=_params("arbitrary"),
        name="ssd_pre",
    )(zs, zs, zs, zs, conv_state, cw, cb, dtb, alog, e_sel)


SAMPLE_TILE = 8


def _ssd_state_kernel(cd_ref, st_ref, xt_ref, bm_ref, cm_ref, xs_ref, z_ref, dsk_ref, gn_ref, *rest):
    so_ref, y_ref, xb_ref, ya_ref = rest[-4:]
    t = pl.program_id(0)
    ya_ref[...] = jnp.zeros(ya_ref.shape, F32)
    x1, x2, x3 = _split3(xt_ref[...])
    seq = lax.broadcasted_iota(jnp.int32, (xt_ref.shape[1], LANES), 0)
    seq_row = lax.broadcasted_iota(jnp.int32, (SAMPLE_TILE, GROUP_CH), 0)

    for j in range(SAMPLE_TILE):
        pick = jnp.where(seq == t * SAMPLE_TILE + j, 1.0, 0.0).astype(BF16)
        xb_ref[...] = _dot(x1, pick) + _dot(x2, pick) + _dot(x3, pick)

        for g in range(N_SSD_GROUPS):
            lanes = slice(g * D_STATE, (g + 1) * D_STATE)
            brow = bm_ref[j:j + 1, lanes]
            news = []
            for r in range(HEADS_PER_SSD_GROUP):
                h = g * HEADS_PER_SSD_GROUP + r
                cd = cd_ref[(t * SAMPLE_TILE + j) * N_SSD_HEADS + h]
                new = st_ref[j, h] * cd + xb_ref[h * SSD_HEAD_DIM:(h + 1) * SSD_HEAD_DIM, :] * brow
                so_ref[j, h] = new
                news.append(new.astype(BF16))
            yg = _dot_nt(cm_ref[:, lanes].astype(BF16), jnp.concatenate(news, axis=0))
            cols = slice(g * GROUP_CH, (g + 1) * GROUP_CH)
            ya_ref[:, cols] += jnp.where(seq_row == j, yg, 0.0)

    y = ya_ref[...] + dsk_ref[...] * xs_ref[...]
    yz = y * _silu(z_ref[...])
    for g in range(N_SSD_GROUPS):
        cols = slice(g * GROUP_CH, (g + 1) * GROUP_CH)
        v = yz[:, cols]
        ms = jnp.mean(v * v, axis=-1, keepdims=True)
        y_ref[:, cols] = v * lax.rsqrt(ms + RMS_EPS) * gn_ref[:, cols]


def _ssd_state(cd_flat, states, layer, stacked, xdt_t, act, zs, dsk_exp, gn):
    bd = states.shape[1]
    nt = bd // SAMPLE_TILE
    const = lambda t: (0, 0)
    st_blk = (None, SAMPLE_TILE, N_SSD_HEADS, SSD_HEAD_DIM, D_STATE)
    in_specs = [
        pl.BlockSpec(memory_space=pltpu.SMEM),
        pl.BlockSpec(st_blk, lambda t: (layer, t, 0, 0, 0)),
        pl.BlockSpec((D_INNER, bd), const),
        pl.BlockSpec((SAMPLE_TILE, 1024), lambda t: (t, D_INNER // 1024)),
        pl.BlockSpec((SAMPLE_TILE, 1024), lambda t: (t, D_INNER // 1024 + 1)),
        pl.BlockSpec((SAMPLE_TILE, D_INNER), lambda t: (t, 0)),
        pl.BlockSpec((SAMPLE_TILE, D_INNER), lambda t: (t, COL_Z // D_INNER)),
        pl.BlockSpec((1, D_INNER), const),
        pl.BlockSpec((1, D_INNER), const),
    ]
    args = [cd_flat, states, xdt_t, act, act, act, zs, dsk_exp, gn]
    aliases = {}
    if stacked is not None:
        in_specs.append(pl.BlockSpec(memory_space=pl.ANY))
        args.append(stacked)
        aliases = {len(args) - 1: 0}
    return pl.pallas_call(
        _ssd_state_kernel,
        grid=(nt,),
        in_specs=in_specs,
        out_specs=[
            pl.BlockSpec(st_blk, lambda t: (layer, t, 0, 0, 0)),
            pl.BlockSpec((SAMPLE_TILE, D_INNER), lambda t: (t, 0)),
        ],
        out_shape=[
            jax.ShapeDtypeStruct(states.shape, F32),
            jax.ShapeDtypeStruct((bd, D_INNER), F32),
        ],
        scratch_shapes=[pltpu.VMEM((D_INNER, LANES), F32), pltpu.VMEM((SAMPLE_TILE, D_INNER), F32)],
        input_output_aliases=aliases,
        compiler_params=_params("arbitrary"),
        name="ssd_state",
    )(*args)


def _attn_sample_kernel(qkv_ref, c0_ref, c1_ref, c2_ref, bc0_ref, bc1_ref, bc2_ref, b0_ref, o_ref,
                        s_ref, p_ref, t_ref):
    outs, lses = [], []
    for g, (c_ref, bc_ref) in enumerate(((c0_ref, bc0_ref), (c1_ref, bc1_ref), (c2_ref, bc2_ref))):
        wc = c_ref.shape[-1]
        q = qkv_ref[g] * ATTN_SCALE
        kn = qkv_ref[N_DIL + g]
        vn = qkv_ref[2 * N_DIL + g]
        t_ref[...] = jnp.zeros(t_ref.shape, F32)
        t_ref[0:HPG, 0:ATTN_HEAD_DIM] = q
        q_t = t_ref[...].T
        for h in range(HPG):
            qcol = q_t[0:ATTN_HEAD_DIM, h:h + 1]
            s_ref[h:h + 1, 0:wc] = jnp.sum(c_ref[0, h] * qcol, axis=0, keepdims=True)
        s = s_ref[0:HPG, 0:wc] + bc_ref[...]
        sn = jnp.sum(kn * q, axis=-1, keepdims=True) + b0_ref[g][:, 0:1]
        m = jnp.maximum(jnp.max(s, axis=-1, keepdims=True), sn)
        p = jnp.exp(s - m)
        pn = jnp.exp(sn - m)
        l = jnp.sum(p, axis=-1, keepdims=True) + pn
        p_ref[0:HPG, 0:wc] = p
        for h in range(HPG):
            t_ref[0:ATTN_HEAD_DIM, h:h + 1] = jnp.sum(c_ref[1, h] * p_ref[h:h + 1, 0:wc], axis=-1, keepdims=True)
        pv = t_ref[...].T[0:HPG, 0:ATTN_HEAD_DIM]
        outs.append((pv + pn * vn) / l)
        lses.append(m + jnp.log(l))
    top = functools.reduce(jnp.maximum, lses)
    es = [jnp.exp(l - top) for l in lses]
    den = functools.reduce(lambda u, v: u + v, es)
    o_ref[...] = functools.reduce(lambda u, v: u + v, [e * og for e, og in zip(es, outs)]) / den


def _attn_sample(qkv, caches_t, layer, bias_c, bias_0):
    bd = qkv.shape[0]
    specs = [pl.BlockSpec((None, None, 2, HPG, ATTN_HEAD_DIM, c.shape[-1]), lambda b: (layer, b, 0, 0, 0, 0))
             for c in caches_t]
    bias_specs = [pl.BlockSpec(bc.shape, lambda b: (0, 0)) for bc in bias_c]
    wc_max = max(c.shape[-1] for c in caches_t)
    return pl.pallas_call(
        _attn_sample_kernel,
        grid=(bd,),
        in_specs=[
            pl.BlockSpec((None, 3 * N_DIL, HPG, ATTN_HEAD_DIM), lambda b: (b, 0, 0, 0)),
            *specs,
            *bias_specs,
            pl.BlockSpec((N_DIL, HPG, ATTN_HEAD_DIM), lambda b: (0, 0, 0)),
        ],
        out_specs=pl.BlockSpec((None, HPG, ATTN_HEAD_DIM), lambda b: (b, 0, 0)),
        out_shape=jax.ShapeDtypeStruct((bd, HPG, ATTN_HEAD_DIM), F32),
        scratch_shapes=[pltpu.VMEM((2 * SUBLANES, wc_max), F32), pltpu.VMEM((2 * SUBLANES, wc_max), F32),
                        pltpu.VMEM((LANES, LANES), F32)],
        compiler_params=_params("parallel"),
        name="attn_sample",
    )(qkv, *caches_t, *bias_c, bias_0)


def _t5_bucket(dist):
    max_exact = N_BUCKETS // 2
    n = np.maximum(dist, 1).astype(np.float32)
    large = max_exact + (np.log(n / np.float32(max_exact)) / np.float32(math.log(MAX_DISTANCE / max_exact))
                         * np.float32(N_BUCKETS - max_exact)).astype(np.int32)
    large = np.minimum(large, N_BUCKETS - 1)
    return np.where(dist < max_exact, dist, large)


def _select_rows(onehot, table):
    return jnp.dot(jnp.asarray(onehot, F32), table.astype(F32), precision=lax.Precision.HIGHEST)


def _group_bias(rel_bias, g):
    _, dil = DILATION_GROUPS[g]
    bucket = _t5_bucket(np.arange(BAND + 1, dtype=np.int32) * dil)
    onehot = bucket[:, None] == np.arange(N_BUCKETS)[None, :]
    return _select_rows(onehot, rel_bias[:, g * HPG:(g + 1) * HPG])


def _prompt_bias(bias):
    period = 3 * BAND
    row = jnp.concatenate([bias[::-1].T, jnp.full((HPG, period - (BAND + 1)), MASKED, F32)], axis=1)
    flat = jnp.tile(row, (1, BAND))[:, :BAND * (period - 1)]
    return flat.reshape(HPG, BAND, period - 1)[:, :, :2 * BAND]


def _cache_bias(bias, g):
    window, dil = DILATION_GROUPS[g]
    w = np.arange(window)
    step = (window - w) // dil
    on_grid = (w % dil == 0)
    onehot = (np.arange(BAND + 1)[:, None] == step[None, :]) & on_grid[None, :]
    table = jnp.dot(bias.T, jnp.asarray(onehot, F32), precision=lax.Precision.HIGHEST)
    return jnp.where(jnp.asarray(on_grid)[None, :], table, MASKED)


def _reorder_w_in(w):
    z, xs, bm, cm, dt, q, k, v, ga, ms, ma = jnp.split(
        w, np.cumsum([2048, 2048, 1024, 1024, 32, 2304, 2304, 2304, 768, 1024])[:].tolist(), axis=-1)
    pad = jnp.zeros((w.shape[0], N_COLS - COL_DT - N_SSD_HEADS), w.dtype)
    return jnp.concatenate([xs, z, bm, cm, q, k, v, ga, ms, ma, dt, pad], axis=-1).astype(BF16)


def _lane_pad(v):
    return jnp.pad(v.astype(F32), (0, LANES - v.shape[0])).reshape(1, LANES)


def kernel(x_prompt, x_sample, cache_kv_w128, cache_kv_w512, cache_kv_w2048, state_conv, state_ssm,
           c_prompt, c_sample, rel_bias, w_ada, b_ada, g_pre, w_in, conv_w, conv_b, dt_bias, a_log,
           d_skip, g_ssd_norm, w_br_ssd, w_br_attn, w_out, g_post):
    depth = w_in.shape[0]
    bp, seq, _ = x_prompt.shape
    bd = x_sample.shape[0]
    assert x_sample.shape[1] == 1 and seq % (BAND * 16) == 0 and bd % SAMPLE_TILE == 0 and bd % LANES == 0
    caches = (cache_kv_w128, cache_kv_w512, cache_kv_w2048)
    assert all(c.shape[2] == w for c, (w, _) in zip(caches, DILATION_GROUPS))
    caches_t = [jnp.transpose(c, (0, 1, 3, 4, 5, 2)) for c in caches]

    head_of_ch = np.arange(D_INNER) // SSD_HEAD_DIM
    e_sel = jnp.asarray(head_of_ch[None, :] == np.arange(LANES)[:, None], BF16)
    tri = jnp.asarray(np.tril(np.ones((SSD_CHUNK, SSD_CHUNK))), BF16)

    gb = [_group_bias(rel_bias, g) for g in range(N_DIL)]
    bias_qk = [_prompt_bias(b) for b in gb]
    bias_c = [_cache_bias(b, g) for g, b in enumerate(gb)]
    bias_0 = jnp.stack([jnp.broadcast_to(b[0][:, None], (HPG, ATTN_HEAD_DIM)) for b in gb])

    n_c = bp + bd
    rows = -(-n_c // SUBLANES) * SUBLANES
    c_all = jnp.pad(jnp.concatenate([c_prompt, c_sample], axis=0), ((0, rows - n_c), (0, 0)))
    mod_all = _ada(c_all, w_ada, b_ada)

    xp = x_prompt
    xs = x_sample.reshape(1, bd, D_MODEL)
    outs = {k: [] for k in ("kvp0", "kvp1", "kvp2", "kvs0", "kvs1", "kvs2", "convp", "convs", "ssmp")}
    ssm_s = None
    for l in range(depth):
        w_l = _reorder_w_in(w_in[l])
        wbs, wba, wo = w_br_ssd[l].astype(BF16), w_br_attn[l].astype(BF16), w_out[l].astype(BF16)
        cw, cb = conv_w[l], conv_b[l].reshape(1, CONV_CH)
        dtb, alog = _lane_pad(dt_bias[l]), _lane_pad(a_log[l])
        dsk_exp = jnp.repeat(d_skip[l].astype(F32), SSD_HEAD_DIM).reshape(1, D_INNER)
        gn = g_ssd_norm[l].reshape(1, D_INNER)
        mod_p = mod_all[l, 0:bp].reshape(bp, 1, 3 * D_MODEL)
        mod_s = mod_all[l, bp:n_c].reshape(1, bd, 3 * D_MODEL)

        zx = _inproj(xp, mod_p, g_pre[l], w_l, tm=2048, tn=ATTN_OUT)
        y_p, ssm_p = _ssd_prompt(zx, cw, cb, dtb, alog, dsk_exp, gn, e_sel, tri)
        attn = [_attn_prompt(zx, bias_qk[g], g) for g in range(N_DIL)]
        xp = _outproj(xp, mod_p, y_p, [a[0] for a in attn], [a[1] for a in attn], zx, wbs, wba, wo,
                      g_post[l], tm=512)
        for g, (window, _) in enumerate(DILATION_GROUPS):
            kk = zx[:, seq - window:, COL_K + g * ATTN_OUT:COL_K + (g + 1) * ATTN_OUT]
            vv = zx[:, seq - window:, COL_V + g * ATTN_OUT:COL_V + (g + 1) * ATTN_OUT]
            outs[f"kvp{g}"].append(jnp.stack([kk, vv], axis=2).reshape(bp, window, 2, HPG, ATTN_HEAD_DIM))
        outs["convp"].append(jnp.concatenate(
            [zx[:, seq - (CONV_W - 1):, COL_XS:COL_XS + D_INNER], zx[:, seq - (CONV_W - 1):, COL_B:COL_B + 2048]],
            axis=-1))
        outs["ssmp"].append(ssm_p)

        zs = _inproj(xs, mod_s, g_pre[l], w_l, tm=bd, tn=ATTN_OUT)[0]
        new_conv, act, xdt_t, cd = _ssd_pre(zs, state_conv[l].reshape(bd, (CONV_W - 1) * CONV_CH),
                                            cw, cb, dtb, alog, e_sel)
        ssm_s, y_s = _ssd_state(cd[:, 0:N_SSD_HEADS].reshape(-1), state_ssm, l, ssm_s, xdt_t, act, zs,
                                dsk_exp, gn)
        qkv = zs[:, COL_Q:COL_Q + 3 * ATTN_QKV].reshape(bd, 3 * N_DIL, HPG, ATTN_HEAD_DIM)
        o_s = _attn_sample(qkv, caches_t, l, bias_c, bias_0)
        xs = _outproj(xs, mod_s, y_s.reshape(1, bd, D_INNER), [o_s.reshape(1, bd, ATTN_OUT)], [],
                      zs.reshape(1, bd, N_COLS), wbs, wba, wo, g_post[l], tm=bd)
        for g in range(N_DIL):
            kk = zs[:, COL_K + g * ATTN_OUT:COL_K + (g + 1) * ATTN_OUT]
            vv = zs[:, COL_V + g * ATTN_OUT:COL_V + (g + 1) * ATTN_OUT]
            outs[f"kvs{g}"].append(jnp.stack([kk, vv], axis=1).reshape(bd, 1, 2, HPG, ATTN_HEAD_DIM))
        outs["convs"].append(new_conv.reshape(bd, CONV_W - 1, CONV_CH))

    st = lambda key: jnp.stack(outs[key], axis=0)
    return (xp, xs.reshape(bd, 1, D_MODEL), st("kvp0"), st("kvs0"), st("kvp1"), st("kvs1"), st("kvp2"), st("kvs2"),
            st("convp"), st("convs"), st("ssmp"), ssm_s)
```

```python
import functools
import math

import numpy as np
import jax
import jax.numpy as jnp
from jax import lax
from jax.experimental import pallas as pl
from jax.experimental.pallas import tpu as pltpu

F32 = jnp.float32
BF16 = jnp.bfloat16

D_MODEL = 1024
D_INNER = 2048
SSD_HEAD_DIM = 64
N_SSD_HEADS = 32
D_STATE = 128
N_SSD_GROUPS = 8
HEADS_PER_SSD_GROUP = 4
GROUP_CH = D_INNER // N_SSD_GROUPS
CONV_W = 4
CONV_CH = 4096
SSD_CHUNK = 128
DILATION_GROUPS = ((128, 1), (512, 4), (2048, 16))
N_DIL = 3
ATTN_HEAD_DIM = 64
HPG = 12
ATTN_QKV = 2304
ATTN_OUT = 768
ATTN_SCALE = ATTN_HEAD_DIM ** -0.5
N_BUCKETS = 32
MAX_DISTANCE = 2048
RMS_EPS = 1e-6
BAND = 128
MASKED = -1e30

LANES = 128
SUBLANES = 8
VMEM_LIMIT = 56 * 1024 * 1024

COL_Z = 0
COL_XS = 2048
COL_B = 4096
COL_C = 5120
COL_Q = 6144
COL_K = 8448
COL_V = 10752
COL_GA = 13056
COL_MS = 13824
COL_MA = 14848
COL_DT = 15872
N_COLS = 16128
QKV_BLOCKS = N_COLS // ATTN_OUT


def _dot(a, b):
    return jnp.dot(a, b, preferred_element_type=F32)


def _dot_nt(a, b):
    return lax.dot_general(a, b, (((1,), (1,)), ((), ())), preferred_element_type=F32)


def _split3(x):
    x1 = x.astype(BF16)
    r1 = x - x1.astype(F32)
    x2 = r1.astype(BF16)
    x3 = (r1 - x2.astype(F32)).astype(BF16)
    return x1, x2, x3


def _dot_sel_rhs(x, sel):
    x1, x2, x3 = _split3(x)
    return _dot(x1, sel) + _dot(x2, sel) + _dot(x3, sel)


def _dot_sel_lhs(sel, x):
    x1, x2, x3 = _split3(x)
    return _dot(sel, x1) + _dot(sel, x2) + _dot(sel, x3)


def _sigmoid(x):
    return 0.5 + 0.5 * jnp.tanh(0.5 * x)


def _silu(x):
    t = 0.5 * x
    return t + t * jnp.tanh(t)


def _softplus(x):
    return jnp.maximum(x, 0.0) + jnp.log1p(jnp.exp(-jnp.abs(x)))


def _params(*sem):
    return pltpu.CompilerParams(dimension_semantics=sem, vmem_limit_bytes=VMEM_LIMIT)


def _ada_kernel(c_ref, w_ref, b_ref, o_ref):
    c = c_ref[...]
    o_ref[...] = _dot(_silu(c).astype(BF16), w_ref[...].astype(BF16)) + b_ref[...]


def _ada(c_all, w_ada, b_ada):
    depth = w_ada.shape[0]
    rows = c_all.shape[0]
    nj = 3 * D_MODEL // 1024
    return pl.pallas_call(
        _ada_kernel,
        grid=(depth, nj),
        in_specs=[
            pl.BlockSpec((rows, D_MODEL), lambda l, j: (0, 0)),
            pl.BlockSpec((None, D_MODEL, 1024), lambda l, j: (l, 0, j)),
            pl.BlockSpec((None, 1, 1024), lambda l, j: (l, 0, j)),
        ],
        out_specs=pl.BlockSpec((None, rows, 1024), lambda l, j: (l, 0, j)),
        out_shape=jax.ShapeDtypeStruct((depth, rows, 3 * D_MODEL), F32),
        compiler_params=_params("parallel", "parallel"),
        name="ada",
    )(c_all, w_ada, b_ada.reshape(depth, 1, 3 * D_MODEL))


COL_TILE = ATTN_OUT
N_COL_TILES = N_COLS // COL_TILE
DT_ROWS = N_SSD_HEADS
FIRST_TILE_AFTER_DT = COL_Q // COL_TILE
W_IN_ROWS = 15904


def _w_tile_row(j):
    last_start = W_IN_ROWS - COL_TILE
    shifted = j * COL_TILE + jnp.where(j >= FIRST_TILE_AFTER_DT, DT_ROWS, 0)
    return jnp.where(j >= N_COL_TILES - 1, last_start, shifted)


def _inproj_kernel(x_ref, mod_ref, g_ref, w_ref, wl_ref, o_ref, h_ref):
    j = pl.program_id(2)

    @pl.when(j == 0)
    def _():
        x = x_ref[...]
        ms = jnp.mean(x * x, axis=-1, keepdims=True)
        y = x * lax.rsqrt(ms + RMS_EPS) * g_ref[...]
        shift = mod_ref[:, 0:D_MODEL]
        scale = mod_ref[:, D_MODEL:2 * D_MODEL]
        h_ref[...] = (y * (1.0 + scale) + shift).astype(BF16)

    @pl.when(j < N_COL_TILES - 1)
    def _():
        o_ref[...] = _dot_nt(h_ref[...], w_ref[0].astype(BF16))

    @pl.when(j == N_COL_TILES - 1)
    def _():
        o_ref[...] = _dot_nt(h_ref[...], wl_ref[...].astype(BF16))


def _inproj(x, mod, g_pre, w_t, layer, w_last, tm):
    b, t, _ = x.shape
    tn = COL_TILE
    r = mod.shape[1]
    rb = 1 if r == 1 else tm
    mod_map = (lambda bi, i, j: (bi, 0, 0)) if r == 1 else (lambda bi, i, j: (bi, i, 0))
    return pl.pallas_call(
        _inproj_kernel,
        grid=(b, t // tm, N_COL_TILES),
        in_specs=[
            pl.BlockSpec((None, tm, D_MODEL), lambda bi, i, j: (bi, i, 0)),
            pl.BlockSpec((None, rb, 3 * D_MODEL), mod_map),
            pl.BlockSpec((1, D_MODEL), lambda bi, i, j: (0, 0)),
            pl.BlockSpec((pl.Element(1), pl.Element(tn), pl.Element(D_MODEL)),
                         lambda bi, i, j: (layer, pl.multiple_of(_w_tile_row(j), SUBLANES), 0)),
            pl.BlockSpec((tn, D_MODEL), lambda bi, i, j: (0, 0)),
        ],
        out_specs=pl.BlockSpec((None, tm, tn), lambda bi, i, j: (bi, i, j)),
        out_shape=jax.ShapeDtypeStruct((b, t, N_COLS), F32),
        scratch_shapes=[pltpu.VMEM((tm, D_MODEL), BF16)],
        compiler_params=_params("parallel", "parallel", "arbitrary"),
        name="inproj",
    )(x, mod, g_pre.reshape(1, D_MODEL), w_t, w_last)


def _ssd_prompt_kernel(xs_ref, z_ref, bm_ref, cm_ref, dt_ref, cw_ref, cb_ref, dtb_ref, alog_ref,
                       dsk_ref, gn_ref, e_ref, tri_ref, y_ref, st_out_ref,
                       xpad_ref, act_ref, st_ref):
    c = pl.program_id(1)
    L = SSD_CHUNK

    @pl.when(c == 0)
    def _():
        xpad_ref[0:SUBLANES, :] = jnp.zeros((SUBLANES, CONV_CH), F32)
        st_ref[...] = jnp.zeros(st_ref.shape, F32)

    xpad_ref[SUBLANES:SUBLANES + L, 0:D_INNER] = xs_ref[...]
    xpad_ref[SUBLANES:SUBLANES + L, D_INNER:D_INNER + 1024] = bm_ref[...]
    xpad_ref[SUBLANES:SUBLANES + L, D_INNER + 1024:CONV_CH] = cm_ref[...]

    slab = 512
    for s in range(CONV_CH // slab):
        cols = slice(s * slab, (s + 1) * slab)
        acc = cb_ref[:, cols] + cw_ref[0:1, cols] * xpad_ref[SUBLANES - 3:SUBLANES - 3 + L, cols]
        for w in range(1, CONV_W):
            acc = acc + cw_ref[w:w + 1, cols] * xpad_ref[SUBLANES - 3 + w:SUBLANES - 3 + w + L, cols]
        act_ref[:, cols] = _silu(acc)
    xpad_ref[0:SUBLANES, :] = xpad_ref[L:L + SUBLANES, :]

    dt = _softplus(dt_ref[...] + dtb_ref[...])
    a = -jnp.exp(alog_ref[...])
    acum = _dot_sel_lhs(tri_ref[...], dt * a)
    acum_t = acum.T
    dt_t = dt.T
    a_last = acum[L - 1:L, :]
    w_end = dt * jnp.exp(a_last - acum)
    ea = jnp.exp(acum)
    cd = jnp.broadcast_to(jnp.exp(a_last), (SUBLANES, LANES))
    e_sel = e_ref[...]
    w_exp = _dot_sel_rhs(w_end, e_sel)
    ea_exp = _dot_sel_rhs(ea, e_sel)
    cd_exp = _dot_sel_rhs(cd, e_sel)[0:1, :]

    row = lax.broadcasted_iota(jnp.int32, (L, L), 0)
    col = lax.broadcasted_iota(jnp.int32, (L, L), 1)
    causal = col <= row
    low_half = lax.broadcasted_iota(jnp.int32, (L, LANES), 1) < SSD_HEAD_DIM

    for g in range(N_SSD_GROUPS):
        cols = slice(g * GROUP_CH, (g + 1) * GROUP_CH)
        x_g = act_ref[:, cols]
        b_g = act_ref[:, D_INNER + g * D_STATE:D_INNER + (g + 1) * D_STATE]
        c_g = act_ref[:, D_INNER + 1024 + g * D_STATE:D_INNER + 1024 + (g + 1) * D_STATE]
        xb = x_g.astype(BF16)
        bb = b_g.astype(BF16)
        cbf = c_g.astype(BF16)
        cb = _dot_nt(cbf, bb)
        st = st_ref[g]
        y_g = _dot(cbf, st.astype(BF16)) * ea_exp[:, cols]
        pieces = []
        for pair in range(HEADS_PER_SSD_GROUP // 2):
            xp = xb[:, pair * LANES:(pair + 1) * LANES]
            ys = []
            for k in range(2):
                h = g * HEADS_PER_SSD_GROUP + pair * 2 + k
                seg = acum[:, h:h + 1] - acum_t[h:h + 1, :]
                dec = jnp.exp(jnp.where(causal, seg, -jnp.inf))
                m = (cb * dec * dt_t[h:h + 1, :]).astype(BF16)
                ys.append(_dot(m, xp))
            pieces.append(jnp.where(low_half, ys[0], ys[1]))
        y_g = y_g + jnp.concatenate(pieces, axis=-1) + dsk_ref[:, cols] * x_g
        xw = (x_g * w_exp[:, cols]).astype(BF16)
        st_ref[g] = st * cd_exp[:, cols] + _dot(b_g.T.astype(BF16), xw)
        yz = y_g * _silu(z_ref[:, cols])
        ms = jnp.mean(yz * yz, axis=-1, keepdims=True)
        y_ref[:, cols] = (yz * lax.rsqrt(ms + RMS_EPS) * gn_ref[:, cols]).astype(y_ref.dtype)

    @pl.when(c == pl.num_programs(1) - 1)
    def _():
        for g in range(N_SSD_GROUPS):
            st_out_ref[g * HEADS_PER_SSD_GROUP:(g + 1) * HEADS_PER_SSD_GROUP] = (
                st_ref[g].T.reshape(HEADS_PER_SSD_GROUP, SSD_HEAD_DIM, D_STATE))


def _ssd_prompt(zx, cw, cb, dtb, alog, dsk_exp, gn, e_sel, tri):
    b, s, _ = zx.shape
    nc = s // SSD_CHUNK
    L = SSD_CHUNK
    const = lambda bi, c: (0, 0)
    return pl.pallas_call(
        _ssd_prompt_kernel,
        grid=(b, nc),
        in_specs=[
            pl.BlockSpec((None, L, D_INNER), lambda bi, c: (bi, c, COL_XS // D_INNER)),
            pl.BlockSpec((None, L, D_INNER), lambda bi, c: (bi, c, COL_Z // D_INNER)),
            pl.BlockSpec((None, L, 1024), lambda bi, c: (bi, c, COL_B // 1024)),
            pl.BlockSpec((None, L, 1024), lambda bi, c: (bi, c, COL_C // 1024)),
            pl.BlockSpec((None, L, LANES), lambda bi, c: (bi, c, COL_DT // LANES)),
            pl.BlockSpec((CONV_W, CONV_CH), const),
            pl.BlockSpec((1, CONV_CH), const),
            pl.BlockSpec((1, LANES), const),
            pl.BlockSpec((1, LANES), const),
            pl.BlockSpec((1, D_INNER), const),
            pl.BlockSpec((1, D_INNER), const),
            pl.BlockSpec((LANES, D_INNER), const),
            pl.BlockSpec((L, L), const),
        ],
        out_specs=[
            pl.BlockSpec((None, L, D_INNER), lambda bi, c: (bi, c, 0)),
            pl.BlockSpec((None, N_SSD_HEADS, SSD_HEAD_DIM, D_STATE), lambda bi, c: (bi, 0, 0, 0)),
        ],
        out_shape=[
            jax.ShapeDtypeStruct((b, s, D_INNER), BF16),
            jax.ShapeDtypeStruct((b, N_SSD_HEADS, SSD_HEAD_DIM, D_STATE), F32),
        ],
        scratch_shapes=[
            pltpu.VMEM((L + 2 * SUBLANES, CONV_CH), F32),
            pltpu.VMEM((L, CONV_CH), F32),
            pltpu.VMEM((N_SSD_GROUPS, D_STATE, GROUP_CH), F32),
        ],
        compiler_params=_params("parallel", "arbitrary"),
        name="ssd_prompt",
    )(zx, zx, zx, zx, zx, cw, cb, dtb, alog, dsk_exp, gn, e_sel, tri)


def _head_pair_attention(q, kb, vb, bias_ref, pair_bias0, before_start):
    low = lax.broadcasted_iota(jnp.int32, (1, LANES), 1) < ATTN_HEAD_DIM
    os_, ls_ = [], []
    for k in range(2):
        qk = jnp.where(low if k == 0 else jnp.logical_not(low), q, 0.0).astype(BF16)
        s = jnp.where(before_start, MASKED, _dot_nt(qk, kb) + bias_ref[pair_bias0 + k])
        m = jnp.max(s, axis=-1, keepdims=True)
        p = jnp.exp(s - m)
        l = jnp.sum(p, axis=-1, keepdims=True)
        os_.append(_dot(p.astype(BF16), vb) / l)
        ls_.append(m + jnp.log(l))
    return jnp.where(low, os_[0], os_[1]), jnp.where(low, ls_[0], ls_[1])


DENSE_BLOCKS = 2


def _attn_dense_kernel(q_ref, kc_ref, vc_ref, kp_ref, vp_ref, bias_ref, o_ref, l_ref):
    n = pl.program_id(1)
    kj = lax.broadcasted_iota(jnp.int32, (BAND, 2 * BAND), 1)
    before_start = kj < jnp.where(n == 0, BAND, 0)
    never = kj < 0
    for pair in range(HPG // 2):
        sl = slice(pair * LANES, (pair + 1) * LANES)
        k_all = jnp.concatenate([kp_ref[:, sl], kc_ref[:, sl]], axis=0).astype(BF16)
        v_all = jnp.concatenate([vp_ref[:, sl], vc_ref[:, sl]], axis=0).astype(BF16)
        for i in range(DENSE_BLOCKS):
            rows = slice(i * BAND, (i + 1) * BAND)
            keys = slice(i * BAND, (i + 2) * BAND)
            o, lse = _head_pair_attention(q_ref[rows, sl] * ATTN_SCALE, k_all[keys], v_all[keys], bias_ref,
                                          2 * pair, before_start if i == 0 else never)
            o_ref[rows, sl] = o
            l_ref[rows, sl] = lse


def _attn_dilated_kernel(dil, q_ref, kc_ref, vc_ref, kp_ref, vp_ref, bias_ref, o_ref, l_ref):
    n = pl.program_id(1)
    kj = lax.broadcasted_iota(jnp.int32, (BAND, 2 * BAND), 1)
    before_start = kj < jnp.where(n == 0, BAND, 0)
    for r in range(dil):
        rows = pl.ds(r, BAND, stride=dil)
        kb = jnp.concatenate([kp_ref[rows, :], kc_ref[rows, :]], axis=0).astype(BF16)
        vb = jnp.concatenate([vp_ref[rows, :], vc_ref[rows, :]], axis=0).astype(BF16)
        o, lse = _head_pair_attention(q_ref[rows, :] * ATTN_SCALE, kb, vb, bias_ref, 0, before_start)
        o_ref[rows, :] = o
        l_ref[rows, :] = lse


def _attn_prompt(zx, bias_qk, g):
    b, s, _ = zx.shape
    _, dil = DILATION_GROUPS[g]
    span = BAND * dil
    nb = s // span
    out_shape = [jax.ShapeDtypeStruct((b, s, ATTN_OUT), F32)] * 2
    if dil == 1:
        qb, kb, vb = COL_Q // ATTN_OUT + g, COL_K // ATTN_OUT + g, COL_V // ATTN_OUT + g
        blk = (None, DENSE_BLOCKS * BAND, ATTN_OUT)
        pblk = (None, BAND, ATTN_OUT)
        cur = lambda off: (lambda bi, n: (bi, n, off))
        prev = lambda off: (lambda bi, n: (bi, jnp.maximum(DENSE_BLOCKS * n - 1, 0), off))
        out_blk = pl.BlockSpec(blk, lambda bi, n: (bi, n, 0))
        return pl.pallas_call(
            _attn_dense_kernel,
            grid=(b, nb // DENSE_BLOCKS),
            in_specs=[pl.BlockSpec(blk, cur(qb)), pl.BlockSpec(blk, cur(kb)), pl.BlockSpec(blk, cur(vb)),
                      pl.BlockSpec(pblk, prev(kb)), pl.BlockSpec(pblk, prev(vb)),
                      pl.BlockSpec((HPG, BAND, 2 * BAND), lambda bi, n: (0, 0, 0))],
            out_specs=[out_blk, out_blk],
            out_shape=out_shape,
            compiler_params=_params("parallel", "arbitrary"),
            name=f"attn_prompt_g{g}",
        )(zx, zx, zx, zx, zx, bias_qk)
    pairs = HPG // 2
    qb, kb, vb = [(c + g * ATTN_OUT) // LANES for c in (COL_Q, COL_K, COL_V)]
    blk = (None, span, LANES)
    cur = lambda off: (lambda bi, n, hp: (bi, n, off + hp))
    prev = lambda off: (lambda bi, n, hp: (bi, jnp.maximum(n - 1, 0), off + hp))
    out_blk = pl.BlockSpec(blk, lambda bi, n, hp: (bi, n, hp))
    return pl.pallas_call(
        functools.partial(_attn_dilated_kernel, dil),
        grid=(b, nb, pairs),
        in_specs=[pl.BlockSpec(blk, cur(qb)), pl.BlockSpec(blk, cur(kb)), pl.BlockSpec(blk, cur(vb)),
                  pl.BlockSpec(blk, prev(kb)), pl.BlockSpec(blk, prev(vb)),
                  pl.BlockSpec((2, BAND, 2 * BAND), lambda bi, n, hp: (hp, 0, 0))],
        out_specs=[out_blk, out_blk],
        out_shape=out_shape,
        compiler_params=_params("parallel", "arbitrary", "arbitrary"),
        name=f"attn_prompt_g{g}",
    )(zx, zx, zx, zx, zx, bias_qk)


def _kv_rows_kernel(*refs):
    ins, outs = refs[0:2 * N_DIL], refs[-N_DIL:]
    for g in range(N_DIL):
        for kv in range(2):
            t = ins[2 * g + kv][...].T
            outs[g][kv] = t.reshape(2, ATTN_HEAD_DIM, t.shape[-1])


def _kv_rows(zx, layer, depth, stacked):
    b, s, _ = zx.shape
    in_specs, args, out_specs, out_shape = [], [], [], []
    for g, (window, _) in enumerate(DILATION_GROUPS):
        for col in (COL_K, COL_V):
            c0 = (col + g * ATTN_OUT) // LANES
            in_specs.append(pl.BlockSpec((None, window, LANES),
                                         lambda bi, hp, c0=c0, rb=s // window - 1: (bi, rb, c0 + hp)))
            args.append(zx)
        out_specs.append(pl.BlockSpec((None, None, 2, 2, ATTN_HEAD_DIM, window),
                                      lambda bi, hp: (layer, bi, 0, hp, 0, 0)))
        out_shape.append(jax.ShapeDtypeStruct((depth, b, 2, HPG, ATTN_HEAD_DIM, window), F32))
    aliases = {}
    if stacked is not None:
        for g in range(N_DIL):
            in_specs.append(pl.BlockSpec(memory_space=pl.ANY))
            args.append(stacked[g])
            aliases[len(args) - 1] = g
    return pl.pallas_call(
        _kv_rows_kernel,
        grid=(b, HPG // 2),
        in_specs=in_specs,
        out_specs=out_specs,
        out_shape=out_shape,
        input_output_aliases=aliases,
        compiler_params=_params("parallel", "parallel"),
        name="kv_rows",
    )(*args)


def _outproj_kernel(n_groups, *refs):
    x_ref, gate_ref, y_ref = refs[0:3]
    o_refs = refs[3:3 + n_groups]
    l_refs = refs[3 + n_groups:3 + 2 * n_groups] if n_groups > 1 else ()
    rest = refs[3 + (2 * n_groups if n_groups > 1 else 1):]
    ga_ref, ms0_ref, ms1_ref, ma0_ref, ma1_ref, wbs_ref, wba_ref, wo_ref, gp_ref, out_ref = rest

    if n_groups > 1:
        lses = [r[...] for r in l_refs]
        top = functools.reduce(jnp.maximum, lses)
        es = [jnp.exp(l - top) for l in lses]
        num = functools.reduce(lambda u, v: u + v, [e * r[...] for e, r in zip(es, o_refs)])
        o = num / functools.reduce(lambda u, v: u + v, es)
    else:
        o = o_refs[0][...]
    o = o * _silu(ga_ref[...])
    br_s = _dot(y_ref[...].astype(BF16), wbs_ref[...])
    br_a = _dot(o.astype(BF16), wba_ref[...])
    m_ssd = jnp.concatenate([ms0_ref[...], ms1_ref[...]], axis=-1)
    m_attn = jnp.concatenate([ma0_ref[...], ma1_ref[...]], axis=-1)
    merged = _sigmoid(m_ssd) * br_s + _sigmoid(m_attn) * br_a
    out = _dot(merged.astype(BF16), wo_ref[...])
    ms = jnp.mean(out * out, axis=-1, keepdims=True)
    normed = out * lax.rsqrt(ms + RMS_EPS) * gp_ref[...]
    out_ref[...] = x_ref[...] + gate_ref[...] * normed


def _outproj(x, mod, y, os_, ls_, zx, wbs, wba, wo, g_post, tm):
    b, t, _ = x.shape
    r = mod.shape[1]
    rb = 1 if r == 1 else tm
    n_groups = len(os_)
    row = lambda cb_: (lambda bi, i: (bi, i, cb_))
    mod_map = (lambda bi, i: (bi, 0, 2)) if r == 1 else (lambda bi, i: (bi, i, 2))
    const = lambda bi, i: (0, 0)
    in_specs = [
        pl.BlockSpec((None, tm, D_MODEL), row(0)),
        pl.BlockSpec((None, rb, D_MODEL), mod_map),
        pl.BlockSpec((None, tm, D_INNER), row(0)),
    ]
    in_specs += [pl.BlockSpec((None, tm, ATTN_OUT), row(0)) for _ in range(n_groups + len(ls_))]
    in_specs += [
        pl.BlockSpec((None, tm, ATTN_OUT), row(COL_GA // ATTN_OUT)),
        pl.BlockSpec((None, tm, 512), row(COL_MS // 512)),
        pl.BlockSpec((None, tm, 512), row(COL_MS // 512 + 1)),
        pl.BlockSpec((None, tm, 512), row(COL_MA // 512)),
        pl.BlockSpec((None, tm, 512), row(COL_MA // 512 + 1)),
        pl.BlockSpec((D_INNER, D_MODEL), const),
        pl.BlockSpec((ATTN_OUT, D_MODEL), const),
        pl.BlockSpec((D_MODEL, D_MODEL), const),
        pl.BlockSpec((1, D_MODEL), const),
    ]
    return pl.pallas_call(
        functools.partial(_outproj_kernel, n_groups),
        grid=(b, t // tm),
        in_specs=in_specs,
        out_specs=pl.BlockSpec((None, tm, D_MODEL), row(0)),
        out_shape=jax.ShapeDtypeStruct((b, t, D_MODEL), F32),
        compiler_params=_params("parallel", "parallel"),
        name="outproj",
    )(x, mod, y, *os_, *ls_, zx, zx, zx, zx, zx, wbs, wba, wo, g_post.reshape(1, D_MODEL))


def _ssd_pre_kernel(xs_ref, bm_ref, cm_ref, dt_ref, cs_ref, cw_ref, cb_ref, dtb_ref, alog_ref, e_ref,
                    conv_ref, act_ref, xdt_t_ref, cd_ref):
    parts = ((xs_ref, 0, D_INNER), (bm_ref, D_INNER, 1024), (cm_ref, D_INNER + 1024, 1024))
    for ref, off, width in parts:
        cols = slice(off, off + width)
        raw = ref[...]
        acc = cb_ref[:, cols] + cw_ref[CONV_W - 1:CONV_W, cols] * raw
        for w in range(CONV_W - 1):
            acc = acc + cw_ref[w:w + 1, cols] * cs_ref[w, :, cols]
        act_ref[:, cols] = _silu(acc)
        for w in range(1, CONV_W - 1):
            conv_ref[w - 1, :, cols] = cs_ref[w, :, cols]
        conv_ref[CONV_W - 2, :, cols] = raw
    dt = _softplus(dt_ref[...] + dtb_ref[...])
    a = -jnp.exp(alog_ref[...])
    cd_ref[...] = jnp.exp(dt * a)
    xdt = act_ref[:, 0:D_INNER] * _dot_sel_rhs(dt, e_ref[...])
    xdt_t_ref[...] = xdt.T


def _ssd_pre(zs, conv_states_t, layer, cw, cb, dtb, alog, e_sel):
    bd = zs.shape[0]
    const = lambda i: (0, 0)
    taps = (CONV_W - 1, bd, CONV_CH)
    return pl.pallas_call(
        _ssd_pre_kernel,
        grid=(1,),
        in_specs=[
            pl.BlockSpec((bd, D_INNER), lambda i: (0, COL_XS // D_INNER)),
            pl.BlockSpec((bd, 1024), lambda i: (0, COL_B // 1024)),
            pl.BlockSpec((bd, 1024), lambda i: (0, COL_C // 1024)),
            pl.BlockSpec((bd, LANES), lambda i: (0, COL_DT // LANES)),
            pl.BlockSpec((None,) + taps, lambda i: (layer, 0, 0, 0)),
            pl.BlockSpec((CONV_W, CONV_CH), const),
            pl.BlockSpec((1, CONV_CH), const),
            pl.BlockSpec((1, LANES), const),
            pl.BlockSpec((1, LANES), const),
            pl.BlockSpec((LANES, D_INNER), const),
        ],
        out_specs=[
            pl.BlockSpec(taps, lambda i: (0, 0, 0)),
            pl.BlockSpec((bd, CONV_CH), const),
            pl.BlockSpec((D_INNER, bd), const),
            pl.BlockSpec((bd, LANES), const),
        ],
        out_shape=[
            jax.ShapeDtypeStruct(taps, F32),
            jax.ShapeDtypeStruct((bd, CONV_CH), F32),
            jax.ShapeDtypeStruct((D_INNER, bd), F32),
            jax.ShapeDtypeStruct((bd, LANES), F32),
        ],
        compiler_params=_params("arbitrary"),
        name="ssd_pre",
    )(zs, zs, zs, zs, conv_states_t, cw, cb, dtb, alog, e_sel)


SAMPLE_TILE = 8


def _ssd_state_kernel(cd_ref, st_ref, xt_ref, bm_ref, cm_ref, xs_ref, z_ref, dsk_ref, gn_ref, *rest):
    so_ref, y_ref, xb_ref, ya_ref = rest[-4:]
    t = pl.program_id(0)
    ya_ref[...] = jnp.zeros(ya_ref.shape, F32)
    x1, x2, x3 = _split3(xt_ref[...])
    seq = lax.broadcasted_iota(jnp.int32, (xt_ref.shape[1], LANES), 0)
    seq_row = lax.broadcasted_iota(jnp.int32, (SAMPLE_TILE, GROUP_CH), 0)

    for j in range(SAMPLE_TILE):
        pick = jnp.where(seq == t * SAMPLE_TILE + j, 1.0, 0.0).astype(BF16)
        xb_ref[...] = _dot(x1, pick) + _dot(x2, pick) + _dot(x3, pick)

        for g in range(N_SSD_GROUPS):
            lanes = slice(g * D_STATE, (g + 1) * D_STATE)
            brow = bm_ref[j:j + 1, lanes]
            news = []
            for r in range(HEADS_PER_SSD_GROUP):
                h = g * HEADS_PER_SSD_GROUP + r
                cd = cd_ref[(t * SAMPLE_TILE + j) * N_SSD_HEADS + h]
                new = st_ref[j, h] * cd + xb_ref[h * SSD_HEAD_DIM:(h + 1) * SSD_HEAD_DIM, :] * brow
                so_ref[j, h] = new
                news.append(new.astype(BF16))
            yg = _dot_nt(cm_ref[:, lanes].astype(BF16), jnp.concatenate(news, axis=0))
            cols = slice(g * GROUP_CH, (g + 1) * GROUP_CH)
            ya_ref[:, cols] += jnp.where(seq_row == j, yg, 0.0)

    y = ya_ref[...] + dsk_ref[...] * xs_ref[...]
    yz = y * _silu(z_ref[...])
    for g in range(N_SSD_GROUPS):
        cols = slice(g * GROUP_CH, (g + 1) * GROUP_CH)
        v = yz[:, cols]
        ms = jnp.mean(v * v, axis=-1, keepdims=True)
        y_ref[:, cols] = v * lax.rsqrt(ms + RMS_EPS) * gn_ref[:, cols]


def _ssd_state(cd_flat, states, layer, stacked, xdt_t, act, zs, dsk_exp, gn):
    bd = states.shape[1]
    nt = bd // SAMPLE_TILE
    const = lambda t: (0, 0)
    st_blk = (None, SAMPLE_TILE, N_SSD_HEADS, SSD_HEAD_DIM, D_STATE)
    in_specs = [
        pl.BlockSpec(memory_space=pltpu.SMEM),
        pl.BlockSpec(st_blk, lambda t: (layer, t, 0, 0, 0)),
        pl.BlockSpec((D_INNER, bd), const),
        pl.BlockSpec((SAMPLE_TILE, 1024), lambda t: (t, D_INNER // 1024)),
        pl.BlockSpec((SAMPLE_TILE, 1024), lambda t: (t, D_INNER // 1024 + 1)),
        pl.BlockSpec((SAMPLE_TILE, D_INNER), lambda t: (t, 0)),
        pl.BlockSpec((SAMPLE_TILE, D_INNER), lambda t: (t, COL_Z // D_INNER)),
        pl.BlockSpec((1, D_INNER), const),
        pl.BlockSpec((1, D_INNER), const),
    ]
    args = [cd_flat, states, xdt_t, act, act, act, zs, dsk_exp, gn]
    aliases = {}
    if stacked is not None:
        in_specs.append(pl.BlockSpec(memory_space=pl.ANY))
        args.append(stacked)
        aliases = {len(args) - 1: 0}
    return pl.pallas_call(
        _ssd_state_kernel,
        grid=(nt,),
        in_specs=in_specs,
        out_specs=[
            pl.BlockSpec(st_blk, lambda t: (layer, t, 0, 0, 0)),
            pl.BlockSpec((SAMPLE_TILE, D_INNER), lambda t: (t, 0)),
        ],
        out_shape=[
            jax.ShapeDtypeStruct(states.shape, F32),
            jax.ShapeDtypeStruct((bd, D_INNER), F32),
        ],
        scratch_shapes=[pltpu.VMEM((D_INNER, LANES), F32), pltpu.VMEM((SAMPLE_TILE, D_INNER), F32)],
        input_output_aliases=aliases,
        compiler_params=_params("arbitrary"),
        name="ssd_state",
    )(*args)


def _attn_sample_kernel(qkv_ref, c0_ref, c1_ref, c2_ref, bc0_ref, bc1_ref, bc2_ref, b0_ref, o_ref,
                        s_ref, p_ref, t_ref):
    outs, lses = [], []
    for g, (c_ref, bc_ref) in enumerate(((c0_ref, bc0_ref), (c1_ref, bc1_ref), (c2_ref, bc2_ref))):
        wc = c_ref.shape[-1]
        q = qkv_ref[g] * ATTN_SCALE
        kn = qkv_ref[N_DIL + g]
        vn = qkv_ref[2 * N_DIL + g]
        t_ref[...] = jnp.zeros(t_ref.shape, F32)
        t_ref[0:HPG, 0:ATTN_HEAD_DIM] = q
        q_t = t_ref[...].T
        for h in range(HPG):
            qcol = q_t[0:ATTN_HEAD_DIM, h:h + 1]
            s_ref[h:h + 1, 0:wc] = jnp.sum(c_ref[0, h] * qcol, axis=0, keepdims=True)
        s = s_ref[0:HPG, 0:wc] + bc_ref[...]
        sn = jnp.sum(kn * q, axis=-1, keepdims=True) + b0_ref[g][:, 0:1]
        m = jnp.maximum(jnp.max(s, axis=-1, keepdims=True), sn)
        p = jnp.exp(s - m)
        pn = jnp.exp(sn - m)
        l = jnp.sum(p, axis=-1, keepdims=True) + pn
        p_ref[0:HPG, 0:wc] = p
        for h in range(HPG):
            t_ref[0:ATTN_HEAD_DIM, h:h + 1] = jnp.sum(c_ref[1, h] * p_ref[h:h + 1, 0:wc], axis=-1, keepdims=True)
        pv = t_ref[...].T[0:HPG, 0:ATTN_HEAD_DIM]
        outs.append((pv + pn * vn) / l)
        lses.append(m + jnp.log(l))
    top = functools.reduce(jnp.maximum, lses)
    es = [jnp.exp(l - top) for l in lses]
    den = functools.reduce(lambda u, v: u + v, es)
    o_ref[...] = functools.reduce(lambda u, v: u + v, [e * og for e, og in zip(es, outs)]) / den


def _attn_sample(qkv, caches_t, layer, bias_c, bias_0):
    bd = qkv.shape[0]
    specs = [pl.BlockSpec((None, None, 2, HPG, ATTN_HEAD_DIM, c.shape[-1]), lambda b: (layer, b, 0, 0, 0, 0))
             for c in caches_t]
    bias_specs = [pl.BlockSpec(bc.shape, lambda b: (0, 0)) for bc in bias_c]
    wc_max = max(c.shape[-1] for c in caches_t)
    return pl.pallas_call(
        _attn_sample_kernel,
        grid=(bd,),
        in_specs=[
            pl.BlockSpec((None, 3 * N_DIL, HPG, ATTN_HEAD_DIM), lambda b: (b, 0, 0, 0)),
            *specs,
            *bias_specs,
            pl.BlockSpec((N_DIL, HPG, ATTN_HEAD_DIM), lambda b: (0, 0, 0)),
        ],
        out_specs=pl.BlockSpec((None, HPG, ATTN_HEAD_DIM), lambda b: (b, 0, 0)),
        out_shape=jax.ShapeDtypeStruct((bd, HPG, ATTN_HEAD_DIM), F32),
        scratch_shapes=[pltpu.VMEM((2 * SUBLANES, wc_max), F32), pltpu.VMEM((2 * SUBLANES, wc_max), F32),
                        pltpu.VMEM((LANES, LANES), F32)],
        compiler_params=_params("parallel"),
        name="attn_sample",
    )(qkv, *caches_t, *bias_c, bias_0)


def _t5_bucket(dist):
    max_exact = N_BUCKETS // 2
    n = np.maximum(dist, 1).astype(np.float32)
    large = max_exact + (np.log(n / np.float32(max_exact)) / np.float32(math.log(MAX_DISTANCE / max_exact))
                         * np.float32(N_BUCKETS - max_exact)).astype(np.int32)
    large = np.minimum(large, N_BUCKETS - 1)
    return np.where(dist < max_exact, dist, large)


def _select_rows(onehot, table):
    return jnp.dot(jnp.asarray(onehot, F32), table.astype(F32), precision=lax.Precision.HIGHEST)


def _group_bias(rel_bias, g):
    _, dil = DILATION_GROUPS[g]
    bucket = _t5_bucket(np.arange(BAND + 1, dtype=np.int32) * dil)
    onehot = bucket[:, None] == np.arange(N_BUCKETS)[None, :]
    return _select_rows(onehot, rel_bias[:, g * HPG:(g + 1) * HPG])


def _prompt_bias(bias):
    period = 3 * BAND
    row = jnp.concatenate([bias[::-1].T, jnp.full((HPG, period - (BAND + 1)), MASKED, F32)], axis=1)
    flat = jnp.tile(row, (1, BAND))[:, :BAND * (period - 1)]
    return flat.reshape(HPG, BAND, period - 1)[:, :, :2 * BAND]


def _cache_bias(bias, g):
    window, dil = DILATION_GROUPS[g]
    w = np.arange(window)
    step = (window - w) // dil
    on_grid = (w % dil == 0)
    onehot = (np.arange(BAND + 1)[:, None] == step[None, :]) & on_grid[None, :]
    table = jnp.dot(bias.T, jnp.asarray(onehot, F32), precision=lax.Precision.HIGHEST)
    return jnp.where(jnp.asarray(on_grid)[None, :], table, MASKED)


def _last_w_tile(w_t):
    tail = w_t[(N_COL_TILES - 1) * COL_TILE + DT_ROWS:]
    dt = w_t[D_INNER + CONV_CH:D_INNER + CONV_CH + DT_ROWS]
    pad = jnp.zeros((COL_TILE - tail.shape[0] - DT_ROWS, D_MODEL), w_t.dtype)
    return jnp.concatenate([tail, dt, pad], axis=0)


def _lane_pad(v):
    return jnp.pad(v.astype(F32), (0, LANES - v.shape[0])).reshape(1, LANES)


def kernel(x_prompt, x_sample, cache_kv_w128, cache_kv_w512, cache_kv_w2048, state_conv, state_ssm,
           c_prompt, c_sample, rel_bias, w_ada, b_ada, g_pre, w_in, conv_w, conv_b, dt_bias, a_log,
           d_skip, g_ssd_norm, w_br_ssd, w_br_attn, w_out, g_post):
    depth = w_in.shape[0]
    bp, seq, _ = x_prompt.shape
    bd = x_sample.shape[0]
    assert x_sample.shape[1] == 1 and seq % (BAND * 16) == 0 and bd % SAMPLE_TILE == 0 and bd % LANES == 0
    caches = (cache_kv_w128, cache_kv_w512, cache_kv_w2048)
    assert all(c.shape[2] == w for c, (w, _) in zip(caches, DILATION_GROUPS))
    caches_t = [jnp.transpose(c, (0, 1, 3, 4, 5, 2)) for c in caches]
    assert w_in.shape[-1] == W_IN_ROWS
    w_in_t = jnp.swapaxes(w_in, 1, 2)
    conv_states_t = jnp.transpose(state_conv, (0, 2, 1, 3))

    head_of_ch = np.arange(D_INNER) // SSD_HEAD_DIM
    e_sel = jnp.asarray(head_of_ch[None, :] == np.arange(LANES)[:, None], BF16)
    tri = jnp.asarray(np.tril(np.ones((SSD_CHUNK, SSD_CHUNK))), BF16)

    gb = [_group_bias(rel_bias, g) for g in range(N_DIL)]
    bias_qk = [_prompt_bias(b) for b in gb]
    bias_c = [_cache_bias(b, g) for g, b in enumerate(gb)]
    bias_0 = jnp.stack([jnp.broadcast_to(b[0][:, None], (HPG, ATTN_HEAD_DIM)) for b in gb])

    n_c = bp + bd
    rows = -(-n_c // SUBLANES) * SUBLANES
    c_all = jnp.pad(jnp.concatenate([c_prompt, c_sample], axis=0), ((0, rows - n_c), (0, 0)))
    mod_all = _ada(c_all, w_ada, b_ada)

    xp = x_prompt
    xs = x_sample.reshape(1, bd, D_MODEL)
    outs = {k: [] for k in ("kvs0", "kvs1", "kvs2", "convp", "convs", "ssmp")}
    ssm_s = None
    kv_p = None
    for l in range(depth):
        w_last = _last_w_tile(w_in_t[l])
        wbs, wba, wo = w_br_ssd[l].astype(BF16), w_br_attn[l].astype(BF16), w_out[l].astype(BF16)
        cw, cb = conv_w[l], conv_b[l].reshape(1, CONV_CH)
        dtb, alog = _lane_pad(dt_bias[l]), _lane_pad(a_log[l])
        dsk_exp = jnp.repeat(d_skip[l].astype(F32), SSD_HEAD_DIM).reshape(1, D_INNER)
        gn = g_ssd_norm[l].reshape(1, D_INNER)
        mod_p = mod_all[l, 0:bp].reshape(bp, 1, 3 * D_MODEL)
        mod_s = mod_all[l, bp:n_c].reshape(1, bd, 3 * D_MODEL)

        zx = _inproj(xp, mod_p, g_pre[l], w_in_t, l, w_last, tm=2048)
        y_p, ssm_p = _ssd_prompt(zx, cw, cb, dtb, alog, dsk_exp, gn, e_sel, tri)
        attn = [_attn_prompt(zx, bias_qk[g], g) for g in range(N_DIL)]
        xp = _outproj(xp, mod_p, y_p, [a[0] for a in attn], [a[1] for a in attn], zx, wbs, wba, wo,
                      g_post[l], tm=512)
        kv_p = _kv_rows(zx, l, depth, kv_p)
        outs["convp"].append(jnp.concatenate(
            [zx[:, seq - (CONV_W - 1):, COL_XS:COL_XS + D_INNER], zx[:, seq - (CONV_W - 1):, COL_B:COL_B + 2048]],
            axis=-1))
        outs["ssmp"].append(ssm_p)

        zs = _inproj(xs, mod_s, g_pre[l], w_in_t, l, w_last, tm=bd)[0]
        new_conv, act, xdt_t, cd = _ssd_pre(zs, conv_states_t, l, cw, cb, dtb, alog, e_sel)
        ssm_s, y_s = _ssd_state(cd[:, 0:N_SSD_HEADS].reshape(-1), state_ssm, l, ssm_s, xdt_t, act, zs,
                                dsk_exp, gn)
        qkv = zs[:, COL_Q:COL_Q + 3 * ATTN_QKV].reshape(bd, 3 * N_DIL, HPG, ATTN_HEAD_DIM)
        o_s = _attn_sample(qkv, caches_t, l, bias_c, bias_0)
        xs = _outproj(xs, mod_s, y_s.reshape(1, bd, D_INNER), [o_s.reshape(1, bd, ATTN_OUT)], [],
                      zs.reshape(1, bd, N_COLS), wbs, wba, wo, g_post[l], tm=bd)
        for g in range(N_DIL):
            kk = zs[:, COL_K + g * ATTN_OUT:COL_K + (g + 1) * ATTN_OUT]
            vv = zs[:, COL_V + g * ATTN_OUT:COL_V + (g + 1) * ATTN_OUT]
            outs[f"kvs{g}"].append(jnp.stack([kk, vv], axis=1).reshape(bd, 1, 2, HPG, ATTN_HEAD_DIM))
        outs["convs"].append(new_conv)

    st = lambda key: jnp.stack(outs[key], axis=0)
    kvp = [jnp.transpose(t, (0, 1, 5, 2, 3, 4)) for t in kv_p]
    return (xp, xs.reshape(bd, 1, D_MODEL), kvp[0], st("kvs0"), kvp[1], st("kvs1"), kvp[2], st("kvs2"),
            st("convp"), jnp.transpose(st("convs"), (0, 2, 1, 3)), st("ssmp"), ssm_s)
```

```python
import functools
import math

import numpy as np
import jax
import jax.numpy as jnp
from jax import lax
from jax.experimental import pallas as pl
from jax.experimental.pallas import tpu as pltpu

F32 = jnp.float32
BF16 = jnp.bfloat16

D_MODEL = 1024
D_INNER = 2048
SSD_HEAD_DIM = 64
N_SSD_HEADS = 32
D_STATE = 128
N_SSD_GROUPS = 8
HEADS_PER_SSD_GROUP = 4
GROUP_CH = D_INNER // N_SSD_GROUPS
CONV_W = 4
CONV_CH = 4096
SSD_CHUNK = 128
DILATION_GROUPS = ((128, 1), (512, 4), (2048, 16))
N_DIL = 3
ATTN_HEAD_DIM = 64
HPG = 12
ATTN_QKV = 2304
ATTN_OUT = 768
ATTN_SCALE = ATTN_HEAD_DIM ** -0.5
N_BUCKETS = 32
MAX_DISTANCE = 2048
RMS_EPS = 1e-6
BAND = 128
MASKED = -1e30

LANES = 128
SUBLANES = 8
VMEM_LIMIT = 56 * 1024 * 1024

COL_Z = 0
COL_XS = 2048
COL_B = 4096
COL_C = 5120
COL_Q = 6144
COL_K = 8448
COL_V = 10752
COL_GA = 13056
COL_MS = 13824
COL_MA = 14848
COL_DT = 15872
N_COLS = 16128
QKV_BLOCKS = N_COLS // ATTN_OUT


def _dot(a, b):
    return jnp.dot(a, b, preferred_element_type=F32)


def _dot_nt(a, b):
    return lax.dot_general(a, b, (((1,), (1,)), ((), ())), preferred_element_type=F32)


def _split3(x):
    x1 = x.astype(BF16)
    r1 = x - x1.astype(F32)
    x2 = r1.astype(BF16)
    x3 = (r1 - x2.astype(F32)).astype(BF16)
    return x1, x2, x3


def _dot_sel_rhs(x, sel):
    x1, x2, x3 = _split3(x)
    return _dot(x1, sel) + _dot(x2, sel) + _dot(x3, sel)


def _dot_sel_lhs(sel, x):
    x1, x2, x3 = _split3(x)
    return _dot(sel, x1) + _dot(sel, x2) + _dot(sel, x3)


def _sigmoid(x):
    return 0.5 + 0.5 * jnp.tanh(0.5 * x)


def _silu(x):
    t = 0.5 * x
    return t + t * jnp.tanh(t)


def _softplus(x):
    return jnp.maximum(x, 0.0) + jnp.log1p(jnp.exp(-jnp.abs(x)))


def _params(*sem):
    return pltpu.CompilerParams(dimension_semantics=sem, vmem_limit_bytes=VMEM_LIMIT)


def _ada_kernel(c_ref, w_ref, b_ref, o_ref):
    c = c_ref[...]
    o_ref[...] = _dot(_silu(c).astype(BF16), w_ref[...].astype(BF16)) + b_ref[...]


def _ada(c_all, w_ada, b_ada):
    depth = w_ada.shape[0]
    rows = c_all.shape[0]
    nj = 3 * D_MODEL // 1024
    return pl.pallas_call(
        _ada_kernel,
        grid=(depth, nj),
        in_specs=[
            pl.BlockSpec((rows, D_MODEL), lambda l, j: (0, 0)),
            pl.BlockSpec((None, D_MODEL, 1024), lambda l, j: (l, 0, j)),
            pl.BlockSpec((None, 1, 1024), lambda l, j: (l, 0, j)),
        ],
        out_specs=pl.BlockSpec((None, rows, 1024), lambda l, j: (l, 0, j)),
        out_shape=jax.ShapeDtypeStruct((depth, rows, 3 * D_MODEL), F32),
        compiler_params=_params("parallel", "parallel"),
        name="ada",
    )(c_all, w_ada, b_ada.reshape(depth, 1, 3 * D_MODEL))


COL_TILE = ATTN_OUT
N_COL_TILES = N_COLS // COL_TILE
DT_ROWS = N_SSD_HEADS
FIRST_TILE_AFTER_DT = COL_Q // COL_TILE
W_IN_ROWS = 15904


def _w_tile_row(j):
    last_start = W_IN_ROWS - COL_TILE
    shifted = j * COL_TILE + jnp.where(j >= FIRST_TILE_AFTER_DT, DT_ROWS, 0)
    return jnp.where(j >= N_COL_TILES - 1, last_start, shifted)


def _inproj_kernel(x_ref, mod_ref, g_ref, w_ref, wl_ref, o_ref, h_ref):
    j = pl.program_id(2)

    @pl.when(j == 0)
    def _():
        x = x_ref[...]
        ms = jnp.mean(x * x, axis=-1, keepdims=True)
        y = x * lax.rsqrt(ms + RMS_EPS) * g_ref[...]
        shift = mod_ref[:, 0:D_MODEL]
        scale = mod_ref[:, D_MODEL:2 * D_MODEL]
        h_ref[...] = (y * (1.0 + scale) + shift).astype(BF16)

    @pl.when(j < N_COL_TILES - 1)
    def _():
        o_ref[...] = _dot_nt(h_ref[...], w_ref[0].astype(BF16))

    @pl.when(j == N_COL_TILES - 1)
    def _():
        o_ref[...] = _dot_nt(h_ref[...], wl_ref[...].astype(BF16))


def _inproj(x, mod, g_pre, w_t, layer, w_last, tm):
    b, t, _ = x.shape
    tn = COL_TILE
    r = mod.shape[1]
    rb = 1 if r == 1 else tm
    mod_map = (lambda bi, i, j: (bi, 0, 0)) if r == 1 else (lambda bi, i, j: (bi, i, 0))
    return pl.pallas_call(
        _inproj_kernel,
        grid=(b, t // tm, N_COL_TILES),
        in_specs=[
            pl.BlockSpec((None, tm, D_MODEL), lambda bi, i, j: (bi, i, 0)),
            pl.BlockSpec((None, rb, 3 * D_MODEL), mod_map),
            pl.BlockSpec((1, D_MODEL), lambda bi, i, j: (0, 0)),
            pl.BlockSpec((pl.Element(1), pl.Element(tn), pl.Element(D_MODEL)),
                         lambda bi, i, j: (layer, pl.multiple_of(_w_tile_row(j), SUBLANES), 0)),
            pl.BlockSpec((None, tn, D_MODEL), lambda bi, i, j: (layer, 0, 0)),
        ],
        out_specs=pl.BlockSpec((None, tm, tn), lambda bi, i, j: (bi, i, j)),
        out_shape=jax.ShapeDtypeStruct((b, t, N_COLS), F32),
        scratch_shapes=[pltpu.VMEM((tm, D_MODEL), BF16)],
        compiler_params=_params("parallel", "parallel", "arbitrary"),
        name="inproj",
    )(x, mod, g_pre.reshape(1, D_MODEL), w_t, w_last)


def _ssd_prompt_kernel(xs_ref, z_ref, bm_ref, cm_ref, dt_ref, cw_ref, cb_ref, dtb_ref, alog_ref,
                       dsk_ref, gn_ref, e_ref, tri_ref, y_ref, st_out_ref,
                       xpad_ref, act_ref, st_ref):
    c = pl.program_id(1)
    L = SSD_CHUNK

    @pl.when(c == 0)
    def _():
        xpad_ref[0:SUBLANES, :] = jnp.zeros((SUBLANES, CONV_CH), F32)
        st_ref[...] = jnp.zeros(st_ref.shape, F32)

    xpad_ref[SUBLANES:SUBLANES + L, 0:D_INNER] = xs_ref[...]
    xpad_ref[SUBLANES:SUBLANES + L, D_INNER:D_INNER + 1024] = bm_ref[...]
    xpad_ref[SUBLANES:SUBLANES + L, D_INNER + 1024:CONV_CH] = cm_ref[...]

    slab = 512
    for s in range(CONV_CH // slab):
        cols = slice(s * slab, (s + 1) * slab)
        acc = cb_ref[:, cols] + cw_ref[0:1, cols] * xpad_ref[SUBLANES - 3:SUBLANES - 3 + L, cols]
        for w in range(1, CONV_W):
            acc = acc + cw_ref[w:w + 1, cols] * xpad_ref[SUBLANES - 3 + w:SUBLANES - 3 + w + L, cols]
        act_ref[:, cols] = _silu(acc)
    xpad_ref[0:SUBLANES, :] = xpad_ref[L:L + SUBLANES, :]

    dt = _softplus(dt_ref[...] + dtb_ref[...])
    a = -jnp.exp(alog_ref[...])
    acum = _dot_sel_lhs(tri_ref[...], dt * a)
    acum_t = acum.T
    dt_t = dt.T
    a_last = acum[L - 1:L, :]
    w_end = dt * jnp.exp(a_last - acum)
    ea = jnp.exp(acum)
    cd = jnp.broadcast_to(jnp.exp(a_last), (SUBLANES, LANES))
    e_sel = e_ref[...]
    w_exp = _dot_sel_rhs(w_end, e_sel)
    ea_exp = _dot_sel_rhs(ea, e_sel)
    cd_exp = _dot_sel_rhs(cd, e_sel)[0:1, :]

    row = lax.broadcasted_iota(jnp.int32, (L, L), 0)
    col = lax.broadcasted_iota(jnp.int32, (L, L), 1)
    causal = col <= row
    low_half = lax.broadcasted_iota(jnp.int32, (L, LANES), 1) < SSD_HEAD_DIM

    for g in range(N_SSD_GROUPS):
        cols = slice(g * GROUP_CH, (g + 1) * GROUP_CH)
        x_g = act_ref[:, cols]
        b_g = act_ref[:, D_INNER + g * D_STATE:D_INNER + (g + 1) * D_STATE]
        c_g = act_ref[:, D_INNER + 1024 + g * D_STATE:D_INNER + 1024 + (g + 1) * D_STATE]
        xb = x_g.astype(BF16)
        bb = b_g.astype(BF16)
        cbf = c_g.astype(BF16)
        cb = _dot_nt(cbf, bb)
        st = st_ref[g]
        y_g = _dot(cbf, st.astype(BF16)) * ea_exp[:, cols]
        pieces = []
        for pair in range(HEADS_PER_SSD_GROUP // 2):
            xp = xb[:, pair * LANES:(pair + 1) * LANES]
            ys = []
            for k in range(2):
                h = g * HEADS_PER_SSD_GROUP + pair * 2 + k
                seg = acum[:, h:h + 1] - acum_t[h:h + 1, :]
                dec = jnp.exp(jnp.where(causal, seg, -jnp.inf))
                m = (cb * dec * dt_t[h:h + 1, :]).astype(BF16)
                ys.append(_dot(m, xp))
            pieces.append(jnp.where(low_half, ys[0], ys[1]))
        y_g = y_g + jnp.concatenate(pieces, axis=-1) + dsk_ref[:, cols] * x_g
        xw = (x_g * w_exp[:, cols]).astype(BF16)
        st_ref[g] = st * cd_exp[:, cols] + _dot(b_g.T.astype(BF16), xw)
        yz = y_g * _silu(z_ref[:, cols])
        ms = jnp.mean(yz * yz, axis=-1, keepdims=True)
        y_ref[:, cols] = (yz * lax.rsqrt(ms + RMS_EPS) * gn_ref[:, cols]).astype(y_ref.dtype)

    @pl.when(c == pl.num_programs(1) - 1)
    def _():
        for g in range(N_SSD_GROUPS):
            st_out_ref[g * HEADS_PER_SSD_GROUP:(g + 1) * HEADS_PER_SSD_GROUP] = (
                st_ref[g].T.reshape(HEADS_PER_SSD_GROUP, SSD_HEAD_DIM, D_STATE))


def _ssd_prompt(zx, cw, cb, dtb, alog, dsk_exp, gn, e_sel, tri):
    b, s, _ = zx.shape
    nc = s // SSD_CHUNK
    L = SSD_CHUNK
    const = lambda bi, c: (0, 0)
    return pl.pallas_call(
        _ssd_prompt_kernel,
        grid=(b, nc),
        in_specs=[
            pl.BlockSpec((None, L, D_INNER), lambda bi, c: (bi, c, COL_XS // D_INNER)),
            pl.BlockSpec((None, L, D_INNER), lambda bi, c: (bi, c, COL_Z // D_INNER)),
            pl.BlockSpec((None, L, 1024), lambda bi, c: (bi, c, COL_B // 1024)),
            pl.BlockSpec((None, L, 1024), lambda bi, c: (bi, c, COL_C // 1024)),
            pl.BlockSpec((None, L, LANES), lambda bi, c: (bi, c, COL_DT // LANES)),
            pl.BlockSpec((CONV_W, CONV_CH), const),
            pl.BlockSpec((1, CONV_CH), const),
            pl.BlockSpec((1, LANES), const),
            pl.BlockSpec((1, LANES), const),
            pl.BlockSpec((1, D_INNER), const),
            pl.BlockSpec((1, D_INNER), const),
            pl.BlockSpec((LANES, D_INNER), const),
            pl.BlockSpec((L, L), const),
        ],
        out_specs=[
            pl.BlockSpec((None, L, D_INNER), lambda bi, c: (bi, c, 0)),
            pl.BlockSpec((None, N_SSD_HEADS, SSD_HEAD_DIM, D_STATE), lambda bi, c: (bi, 0, 0, 0)),
        ],
        out_shape=[
            jax.ShapeDtypeStruct((b, s, D_INNER), BF16),
            jax.ShapeDtypeStruct((b, N_SSD_HEADS, SSD_HEAD_DIM, D_STATE), F32),
        ],
        scratch_shapes=[
            pltpu.VMEM((L + 2 * SUBLANES, CONV_CH), F32),
            pltpu.VMEM((L, CONV_CH), F32),
            pltpu.VMEM((N_SSD_GROUPS, D_STATE, GROUP_CH), F32),
        ],
        compiler_params=_params("parallel", "arbitrary"),
        name="ssd_prompt",
    )(zx, zx, zx, zx, zx, cw, cb, dtb, alog, dsk_exp, gn, e_sel, tri)


def _head_pair_attention(q, kb, vb, bias_ref, pair_bias0, no_prev):
    low = lax.broadcasted_iota(jnp.int32, (1, LANES), 1) < ATTN_HEAD_DIM
    if no_prev is not None:
        kj = lax.broadcasted_iota(jnp.int32, (BAND, 2 * BAND), 1)
        hidden = kj < jnp.where(no_prev, BAND, 0)
    os_, ls_ = [], []
    for k in range(2):
        qk = jnp.where(low if k == 0 else jnp.logical_not(low), q, 0.0).astype(BF16)
        s = _dot_nt(qk, kb) + bias_ref[pair_bias0 + k]
        if no_prev is not None:
            s = jnp.where(hidden, MASKED, s)
        m = jnp.max(s, axis=-1, keepdims=True)
        p = jnp.exp(s - m)
        l = jnp.sum(p, axis=-1, keepdims=True)
        os_.append(_dot(p.astype(BF16), vb) / l)
        ls_.append(m + jnp.log(l))
    return jnp.where(low, os_[0], os_[1]), jnp.where(low, ls_[0], ls_[1])


DENSE_BLOCKS = 2


def _attn_dense_kernel(q_ref, kc_ref, vc_ref, kp_ref, vp_ref, bias_ref, o_ref, l_ref):
    first = pl.program_id(1) == 0
    for pair in range(HPG // 2):
        sl = slice(pair * LANES, (pair + 1) * LANES)
        k_all = jnp.concatenate([kp_ref[:, sl], kc_ref[:, sl]], axis=0).astype(BF16)
        v_all = jnp.concatenate([vp_ref[:, sl], vc_ref[:, sl]], axis=0).astype(BF16)
        for i in range(DENSE_BLOCKS):
            rows = slice(i * BAND, (i + 1) * BAND)
            keys = slice(i * BAND, (i + 2) * BAND)
            o, lse = _head_pair_attention(q_ref[rows, sl] * ATTN_SCALE, k_all[keys], v_all[keys], bias_ref,
                                          2 * pair, first if i == 0 else None)
            o_ref[rows, sl] = o
            l_ref[rows, sl] = lse


def _attn_dilated_kernel(dil, q_ref, kc_ref, vc_ref, bias_ref, o_ref, l_ref, kprev_ref, vprev_ref):
    first = pl.program_id(1) == 0
    hp = pl.program_id(2)

    @pl.when(first)
    def _():
        kprev_ref[hp] = jnp.zeros(kprev_ref.shape[1:], BF16)
        vprev_ref[hp] = jnp.zeros(vprev_ref.shape[1:], BF16)

    for r in range(dil):
        rows = pl.ds(r, BAND, stride=dil)
        kc = kc_ref[rows, :].astype(BF16)
        vc = vc_ref[rows, :].astype(BF16)
        kb = jnp.concatenate([kprev_ref[hp, r], kc], axis=0)
        vb = jnp.concatenate([vprev_ref[hp, r], vc], axis=0)
        o, lse = _head_pair_attention(q_ref[rows, :] * ATTN_SCALE, kb, vb, bias_ref, 0, first)
        o_ref[rows, :] = o
        l_ref[rows, :] = lse
        kprev_ref[hp, r] = kc
        vprev_ref[hp, r] = vc


def _attn_prompt(zx, bias_qk, g):
    b, s, _ = zx.shape
    _, dil = DILATION_GROUPS[g]
    span = BAND * dil
    nb = s // span
    out_shape = [jax.ShapeDtypeStruct((b, s, ATTN_OUT), F32)] * 2
    if dil == 1:
        qb, kb, vb = COL_Q // ATTN_OUT + g, COL_K // ATTN_OUT + g, COL_V // ATTN_OUT + g
        blk = (None, DENSE_BLOCKS * BAND, ATTN_OUT)
        pblk = (None, BAND, ATTN_OUT)
        cur = lambda off: (lambda bi, n: (bi, n, off))
        prev = lambda off: (lambda bi, n: (bi, jnp.maximum(DENSE_BLOCKS * n - 1, 0), off))
        out_blk = pl.BlockSpec(blk, lambda bi, n: (bi, n, 0))
        return pl.pallas_call(
            _attn_dense_kernel,
            grid=(b, nb // DENSE_BLOCKS),
            in_specs=[pl.BlockSpec(blk, cur(qb)), pl.BlockSpec(blk, cur(kb)), pl.BlockSpec(blk, cur(vb)),
                      pl.BlockSpec(pblk, prev(kb)), pl.BlockSpec(pblk, prev(vb)),
                      pl.BlockSpec((HPG, BAND, 2 * BAND), lambda bi, n: (0, 0, 0))],
            out_specs=[out_blk, out_blk],
            out_shape=out_shape,
            compiler_params=_params("parallel", "arbitrary"),
            name=f"attn_prompt_g{g}",
        )(zx, zx, zx, zx, zx, bias_qk)
    pairs = HPG // 2
    qb, kb, vb = [(c + g * ATTN_OUT) // LANES for c in (COL_Q, COL_K, COL_V)]
    blk = (None, span, LANES)
    cur = lambda off: (lambda bi, n, hp: (bi, n, off + hp))
    out_blk = pl.BlockSpec(blk, lambda bi, n, hp: (bi, n, hp))
    carry = pltpu.VMEM((pairs, dil, BAND, LANES), BF16)
    return pl.pallas_call(
        functools.partial(_attn_dilated_kernel, dil),
        grid=(b, nb, pairs),
        in_specs=[pl.BlockSpec(blk, cur(qb)), pl.BlockSpec(blk, cur(kb)), pl.BlockSpec(blk, cur(vb)),
                  pl.BlockSpec((2, BAND, 2 * BAND), lambda bi, n, hp: (hp, 0, 0))],
        out_specs=[out_blk, out_blk],
        out_shape=out_shape,
        scratch_shapes=[carry, carry],
        compiler_params=_params("parallel", "arbitrary", "arbitrary"),
        name=f"attn_prompt_g{g}",
    )(zx, zx, zx, bias_qk)


def _kv_rows_kernel(*refs):
    ins, outs = refs[0:2 * N_DIL], refs[-N_DIL:]
    for g in range(N_DIL):
        for kv in range(2):
            t = ins[2 * g + kv][...].T
            outs[g][kv] = t.reshape(2, ATTN_HEAD_DIM, t.shape[-1])


def _kv_rows(zx, layer, depth, stacked):
    b, s, _ = zx.shape
    in_specs, args, out_specs, out_shape = [], [], [], []
    for g, (window, _) in enumerate(DILATION_GROUPS):
        for col in (COL_K, COL_V):
            c0 = (col + g * ATTN_OUT) // LANES
            in_specs.append(pl.BlockSpec((None, window, LANES),
                                         lambda bi, hp, c0=c0, rb=s // window - 1: (bi, rb, c0 + hp)))
            args.append(zx)
        out_specs.append(pl.BlockSpec((None, None, 2, 2, ATTN_HEAD_DIM, window),
                                      lambda bi, hp: (layer, bi, 0, hp, 0, 0)))
        out_shape.append(jax.ShapeDtypeStruct((depth, b, 2, HPG, ATTN_HEAD_DIM, window), F32))
    aliases = {}
    if stacked is not None:
        for g in range(N_DIL):
            in_specs.append(pl.BlockSpec(memory_space=pl.ANY))
            args.append(stacked[g])
            aliases[len(args) - 1] = g
    return pl.pallas_call(
        _kv_rows_kernel,
        grid=(b, HPG // 2),
        in_specs=in_specs,
        out_specs=out_specs,
        out_shape=out_shape,
        input_output_aliases=aliases,
        compiler_params=_params("parallel", "parallel"),
        name="kv_rows",
    )(*args)


def _outproj_kernel(n_groups, *refs):
    x_ref, gate_ref, y_ref = refs[0:3]
    o_refs = refs[3:3 + n_groups]
    l_refs = refs[3 + n_groups:3 + 2 * n_groups] if n_groups > 1 else ()
    rest = refs[3 + (2 * n_groups if n_groups > 1 else 1):]
    ga_ref, ms0_ref, ms1_ref, ma0_ref, ma1_ref, wbs_ref, wba_ref, wo_ref, gp_ref, out_ref = rest

    if n_groups > 1:
        lses = [r[...] for r in l_refs]
        top = functools.reduce(jnp.maximum, lses)
        es = [jnp.exp(l - top) for l in lses]
        num = functools.reduce(lambda u, v: u + v, [e * r[...] for e, r in zip(es, o_refs)])
        o = num / functools.reduce(lambda u, v: u + v, es)
    else:
        o = o_refs[0][...]
    o = o * _silu(ga_ref[...])
    br_s = _dot(y_ref[...].astype(BF16), wbs_ref[...])
    br_a = _dot(o.astype(BF16), wba_ref[...])
    m_ssd = jnp.concatenate([ms0_ref[...], ms1_ref[...]], axis=-1)
    m_attn = jnp.concatenate([ma0_ref[...], ma1_ref[...]], axis=-1)
    merged = _sigmoid(m_ssd) * br_s + _sigmoid(m_attn) * br_a
    out = _dot(merged.astype(BF16), wo_ref[...])
    ms = jnp.mean(out * out, axis=-1, keepdims=True)
    normed = out * lax.rsqrt(ms + RMS_EPS) * gp_ref[...]
    out_ref[...] = x_ref[...] + gate_ref[...] * normed


def _outproj(x, mod, y, os_, ls_, zx, wbs, wba, wo, g_post, tm):
    b, t, _ = x.shape
    r = mod.shape[1]
    rb = 1 if r == 1 else tm
    n_groups = len(os_)
    row = lambda cb_: (lambda bi, i: (bi, i, cb_))
    mod_map = (lambda bi, i: (bi, 0, 2)) if r == 1 else (lambda bi, i: (bi, i, 2))
    const = lambda bi, i: (0, 0)
    in_specs = [
        pl.BlockSpec((None, tm, D_MODEL), row(0)),
        pl.BlockSpec((None, rb, D_MODEL), mod_map),
        pl.BlockSpec((None, tm, D_INNER), row(0)),
    ]
    in_specs += [pl.BlockSpec((None, tm, ATTN_OUT), row(0)) for _ in range(n_groups + len(ls_))]
    in_specs += [
        pl.BlockSpec((None, tm, ATTN_OUT), row(COL_GA // ATTN_OUT)),
        pl.BlockSpec((None, tm, 512), row(COL_MS // 512)),
        pl.BlockSpec((None, tm, 512), row(COL_MS // 512 + 1)),
        pl.BlockSpec((None, tm, 512), row(COL_MA // 512)),
        pl.BlockSpec((None, tm, 512), row(COL_MA // 512 + 1)),
        pl.BlockSpec((D_INNER, D_MODEL), const),
        pl.BlockSpec((ATTN_OUT, D_MODEL), const),
        pl.BlockSpec((D_MODEL, D_MODEL), const),
        pl.BlockSpec((1, D_MODEL), const),
    ]
    return pl.pallas_call(
        functools.partial(_outproj_kernel, n_groups),
        grid=(b, t // tm),
        in_specs=in_specs,
        out_specs=pl.BlockSpec((None, tm, D_MODEL), row(0)),
        out_shape=jax.ShapeDtypeStruct((b, t, D_MODEL), F32),
        compiler_params=_params("parallel", "parallel"),
        name="outproj",
    )(x, mod, y, *os_, *ls_, zx, zx, zx, zx, zx, wbs, wba, wo, g_post.reshape(1, D_MODEL))


def _ssd_pre_kernel(xs_ref, bm_ref, cm_ref, dt_ref, cs_ref, cw_ref, cb_ref, dtb_ref, alog_ref, e_ref,
                    conv_ref, act_ref, xdt_t_ref, cd_ref):
    parts = ((xs_ref, 0, D_INNER), (bm_ref, D_INNER, 1024), (cm_ref, D_INNER + 1024, 1024))
    for ref, off, width in parts:
        cols = slice(off, off + width)
        raw = ref[...]
        acc = cb_ref[:, cols] + cw_ref[CONV_W - 1:CONV_W, cols] * raw
        for w in range(CONV_W - 1):
            acc = acc + cw_ref[w:w + 1, cols] * cs_ref[w, :, cols]
        act_ref[:, cols] = _silu(acc)
        for w in range(1, CONV_W - 1):
            conv_ref[w - 1, :, cols] = cs_ref[w, :, cols]
        conv_ref[CONV_W - 2, :, cols] = raw
    dt = _softplus(dt_ref[...] + dtb_ref[...])
    a = -jnp.exp(alog_ref[...])
    cd_ref[...] = jnp.exp(dt * a)
    xdt = act_ref[:, 0:D_INNER] * _dot_sel_rhs(dt, e_ref[...])
    xdt_t_ref[...] = xdt.T


def _ssd_pre(zs, conv_states_t, layer, cw, cb, dtb, alog, e_sel):
    bd = zs.shape[0]
    const = lambda i: (0, 0)
    taps = (CONV_W - 1, bd, CONV_CH)
    return pl.pallas_call(
        _ssd_pre_kernel,
        grid=(1,),
        in_specs=[
            pl.BlockSpec((bd, D_INNER), lambda i: (0, COL_XS // D_INNER)),
            pl.BlockSpec((bd, 1024), lambda i: (0, COL_B // 1024)),
            pl.BlockSpec((bd, 1024), lambda i: (0, COL_C // 1024)),
            pl.BlockSpec((bd, LANES), lambda i: (0, COL_DT // LANES)),
            pl.BlockSpec((None,) + taps, lambda i: (layer, 0, 0, 0)),
            pl.BlockSpec((CONV_W, CONV_CH), const),
            pl.BlockSpec((1, CONV_CH), const),
            pl.BlockSpec((1, LANES), const),
            pl.BlockSpec((1, LANES), const),
            pl.BlockSpec((LANES, D_INNER), const),
        ],
        out_specs=[
            pl.BlockSpec(taps, lambda i: (0, 0, 0)),
            pl.BlockSpec((bd, CONV_CH), const),
            pl.BlockSpec((D_INNER, bd), const),
            pl.BlockSpec((bd, LANES), const),
        ],
        out_shape=[
            jax.ShapeDtypeStruct(taps, F32),
            jax.ShapeDtypeStruct((bd, CONV_CH), F32),
            jax.ShapeDtypeStruct((D_INNER, bd), F32),
            jax.ShapeDtypeStruct((bd, LANES), F32),
        ],
        compiler_params=_params("arbitrary"),
        name="ssd_pre",
    )(zs, zs, zs, zs, conv_states_t, cw, cb, dtb, alog, e_sel)


SAMPLE_TILE = 8


def _ssd_state_kernel(cd_ref, st_ref, xt_ref, bm_ref, cm_ref, xs_ref, z_ref, dsk_ref, gn_ref, *rest):
    so_ref, y_ref, xb_ref, ya_ref = rest[-4:]
    t = pl.program_id(0)
    ya_ref[...] = jnp.zeros(ya_ref.shape, F32)
    x1, x2, x3 = _split3(xt_ref[...])
    seq = lax.broadcasted_iota(jnp.int32, (xt_ref.shape[1], LANES), 0)
    seq_row = lax.broadcasted_iota(jnp.int32, (SAMPLE_TILE, GROUP_CH), 0)

    for j in range(SAMPLE_TILE):
        pick = jnp.where(seq == t * SAMPLE_TILE + j, 1.0, 0.0).astype(BF16)
        xb_ref[...] = _dot(x1, pick) + _dot(x2, pick) + _dot(x3, pick)

        for g in range(N_SSD_GROUPS):
            lanes = slice(g * D_STATE, (g + 1) * D_STATE)
            brow = bm_ref[j:j + 1, lanes]
            news = []
            for r in range(HEADS_PER_SSD_GROUP):
                h = g * HEADS_PER_SSD_GROUP + r
                cd = cd_ref[(t * SAMPLE_TILE + j) * N_SSD_HEADS + h]
                new = st_ref[j, h] * cd + xb_ref[h * SSD_HEAD_DIM:(h + 1) * SSD_HEAD_DIM, :] * brow
                so_ref[j, h] = new
                news.append(new.astype(BF16))
            yg = _dot_nt(cm_ref[:, lanes].astype(BF16), jnp.concatenate(news, axis=0))
            cols = slice(g * GROUP_CH, (g + 1) * GROUP_CH)
            ya_ref[:, cols] += jnp.where(seq_row == j, yg, 0.0)

    y = ya_ref[...] + dsk_ref[...] * xs_ref[...]
    yz = y * _silu(z_ref[...])
    for g in range(N_SSD_GROUPS):
        cols = slice(g * GROUP_CH, (g + 1) * GROUP_CH)
        v = yz[:, cols]
        ms = jnp.mean(v * v, axis=-1, keepdims=True)
        y_ref[:, cols] = v * lax.rsqrt(ms + RMS_EPS) * gn_ref[:, cols]


def _ssd_state(cd_flat, states, layer, stacked, xdt_t, act, zs, dsk_exp, gn):
    bd = states.shape[1]
    nt = bd // SAMPLE_TILE
    const = lambda t: (0, 0)
    st_blk = (None, SAMPLE_TILE, N_SSD_HEADS, SSD_HEAD_DIM, D_STATE)
    in_specs = [
        pl.BlockSpec(memory_space=pltpu.SMEM),
        pl.BlockSpec(st_blk, lambda t: (layer, t, 0, 0, 0)),
        pl.BlockSpec((D_INNER, bd), const),
        pl.BlockSpec((SAMPLE_TILE, 1024), lambda t: (t, D_INNER // 1024)),
        pl.BlockSpec((SAMPLE_TILE, 1024), lambda t: (t, D_INNER // 1024 + 1)),
        pl.BlockSpec((SAMPLE_TILE, D_INNER), lambda t: (t, 0)),
        pl.BlockSpec((SAMPLE_TILE, D_INNER), lambda t: (t, COL_Z // D_INNER)),
        pl.BlockSpec((1, D_INNER), const),
        pl.BlockSpec((1, D_INNER), const),
    ]
    args = [cd_flat, states, xdt_t, act, act, act, zs, dsk_exp, gn]
    aliases = {}
    if stacked is not None:
        in_specs.append(pl.BlockSpec(memory_space=pl.ANY))
        args.append(stacked)
        aliases = {len(args) - 1: 0}
    return pl.pallas_call(
        _ssd_state_kernel,
        grid=(nt,),
        in_specs=in_specs,
        out_specs=[
            pl.BlockSpec(st_blk, lambda t: (layer, t, 0, 0, 0)),
            pl.BlockSpec((SAMPLE_TILE, D_INNER), lambda t: (t, 0)),
        ],
        out_shape=[
            jax.ShapeDtypeStruct(states.shape, F32),
            jax.ShapeDtypeStruct((bd, D_INNER), F32),
        ],
        scratch_shapes=[pltpu.VMEM((D_INNER, LANES), F32), pltpu.VMEM((SAMPLE_TILE, D_INNER), F32)],
        input_output_aliases=aliases,
        compiler_params=_params("arbitrary"),
        name="ssd_state",
    )(*args)


def _attn_sample_kernel(qkv_ref, c0_ref, c1_ref, c2_ref, bc0_ref, bc1_ref, bc2_ref, b0_ref, o_ref,
                        s_ref, p_ref, t_ref):
    outs, lses = [], []
    for g, (c_ref, bc_ref) in enumerate(((c0_ref, bc0_ref), (c1_ref, bc1_ref), (c2_ref, bc2_ref))):
        wc = c_ref.shape[-1]
        q = qkv_ref[g] * ATTN_SCALE
        kn = qkv_ref[N_DIL + g]
        vn = qkv_ref[2 * N_DIL + g]
        t_ref[...] = jnp.zeros(t_ref.shape, F32)
        t_ref[0:HPG, 0:ATTN_HEAD_DIM] = q
        q_t = t_ref[...].T
        for h in range(HPG):
            qcol = q_t[0:ATTN_HEAD_DIM, h:h + 1]
            s_ref[h:h + 1, 0:wc] = jnp.sum(c_ref[0, h] * qcol, axis=0, keepdims=True)
        s = s_ref[0:HPG, 0:wc] + bc_ref[...]
        sn = jnp.sum(kn * q, axis=-1, keepdims=True) + b0_ref[g][:, 0:1]
        m = jnp.maximum(jnp.max(s, axis=-1, keepdims=True), sn)
        p = jnp.exp(s - m)
        pn = jnp.exp(sn - m)
        l = jnp.sum(p, axis=-1, keepdims=True) + pn
        p_ref[0:HPG, 0:wc] = p
        for h in range(HPG):
            t_ref[0:ATTN_HEAD_DIM, h:h + 1] = jnp.sum(c_ref[1, h] * p_ref[h:h + 1, 0:wc], axis=-1, keepdims=True)
        pv = t_ref[...].T[0:HPG, 0:ATTN_HEAD_DIM]
        outs.append((pv + pn * vn) / l)
        lses.append(m + jnp.log(l))
    top = functools.reduce(jnp.maximum, lses)
    es = [jnp.exp(l - top) for l in lses]
    den = functools.reduce(lambda u, v: u + v, es)
    o_ref[...] = functools.reduce(lambda u, v: u + v, [e * og for e, og in zip(es, outs)]) / den


def _attn_sample(qkv, caches_t, layer, bias_c, bias_0):
    bd = qkv.shape[0]
    specs = [pl.BlockSpec((None, None, 2, HPG, ATTN_HEAD_DIM, c.shape[-1]), lambda b: (layer, b, 0, 0, 0, 0))
             for c in caches_t]
    bias_specs = [pl.BlockSpec(bc.shape, lambda b: (0, 0)) for bc in bias_c]
    wc_max = max(c.shape[-1] for c in caches_t)
    return pl.pallas_call(
        _attn_sample_kernel,
        grid=(bd,),
        in_specs=[
            pl.BlockSpec((None, 3 * N_DIL, HPG, ATTN_HEAD_DIM), lambda b: (b, 0, 0, 0)),
            *specs,
            *bias_specs,
            pl.BlockSpec((N_DIL, HPG, ATTN_HEAD_DIM), lambda b: (0, 0, 0)),
        ],
        out_specs=pl.BlockSpec((None, HPG, ATTN_HEAD_DIM), lambda b: (b, 0, 0)),
        out_shape=jax.ShapeDtypeStruct((bd, HPG, ATTN_HEAD_DIM), F32),
        scratch_shapes=[pltpu.VMEM((2 * SUBLANES, wc_max), F32), pltpu.VMEM((2 * SUBLANES, wc_max), F32),
                        pltpu.VMEM((LANES, LANES), F32)],
        compiler_params=_params("parallel"),
        name="attn_sample",
    )(qkv, *caches_t, *bias_c, bias_0)


def _t5_bucket(dist):
    max_exact = N_BUCKETS // 2
    n = np.maximum(dist, 1).astype(np.float32)
    large = max_exact + (np.log(n / np.float32(max_exact)) / np.float32(math.log(MAX_DISTANCE / max_exact))
                         * np.float32(N_BUCKETS - max_exact)).astype(np.int32)
    large = np.minimum(large, N_BUCKETS - 1)
    return np.where(dist < max_exact, dist, large)


def _select_rows(onehot, table):
    return jnp.dot(jnp.asarray(onehot, F32), table.astype(F32), precision=lax.Precision.HIGHEST)


def _group_bias(rel_bias, g):
    _, dil = DILATION_GROUPS[g]
    bucket = _t5_bucket(np.arange(BAND + 1, dtype=np.int32) * dil)
    onehot = bucket[:, None] == np.arange(N_BUCKETS)[None, :]
    return _select_rows(onehot, rel_bias[:, g * HPG:(g + 1) * HPG])


def _prompt_bias(bias):
    period = 3 * BAND
    row = jnp.concatenate([bias[::-1].T, jnp.full((HPG, period - (BAND + 1)), MASKED, F32)], axis=1)
    flat = jnp.tile(row, (1, BAND))[:, :BAND * (period - 1)]
    return flat.reshape(HPG, BAND, period - 1)[:, :, :2 * BAND]


def _cache_bias(bias, g):
    window, dil = DILATION_GROUPS[g]
    w = np.arange(window)
    step = (window - w) // dil
    on_grid = (w % dil == 0)
    onehot = (np.arange(BAND + 1)[:, None] == step[None, :]) & on_grid[None, :]
    table = jnp.dot(bias.T, jnp.asarray(onehot, F32), precision=lax.Precision.HIGHEST)
    return jnp.where(jnp.asarray(on_grid)[None, :], table, MASKED)


def _last_w_tiles(w_t):
    tail = w_t[:, (N_COL_TILES - 1) * COL_TILE + DT_ROWS:]
    dt = w_t[:, D_INNER + CONV_CH:D_INNER + CONV_CH + DT_ROWS]
    pad = jnp.zeros((w_t.shape[0], COL_TILE - tail.shape[1] - DT_ROWS, D_MODEL), w_t.dtype)
    return jnp.concatenate([tail, dt, pad], axis=1)


def _lane_pad(v):
    return jnp.pad(v.astype(F32), (0, LANES - v.shape[0])).reshape(1, LANES)


def kernel(x_prompt, x_sample, cache_kv_w128, cache_kv_w512, cache_kv_w2048, state_conv, state_ssm,
           c_prompt, c_sample, rel_bias, w_ada, b_ada, g_pre, w_in, conv_w, conv_b, dt_bias, a_log,
           d_skip, g_ssd_norm, w_br_ssd, w_br_attn, w_out, g_post):
    depth = w_in.shape[0]
    bp, seq, _ = x_prompt.shape
    bd = x_sample.shape[0]
    assert x_sample.shape[1] == 1 and seq % (BAND * 16) == 0 and bd % SAMPLE_TILE == 0 and bd % LANES == 0
    caches = (cache_kv_w128, cache_kv_w512, cache_kv_w2048)
    assert all(c.shape[2] == w for c, (w, _) in zip(caches, DILATION_GROUPS))
    caches_t = [jnp.transpose(c, (0, 1, 3, 4, 5, 2)) for c in caches]
    assert w_in.shape[-1] == W_IN_ROWS
    w_in_t = jnp.swapaxes(w_in, 1, 2)
    w_last = _last_w_tiles(w_in_t)
    conv_states_t = jnp.transpose(state_conv, (0, 2, 1, 3))

    head_of_ch = np.arange(D_INNER) // SSD_HEAD_DIM
    e_sel = jnp.asarray(head_of_ch[None, :] == np.arange(LANES)[:, None], BF16)
    tri = jnp.asarray(np.tril(np.ones((SSD_CHUNK, SSD_CHUNK))), BF16)

    gb = [_group_bias(rel_bias, g) for g in range(N_DIL)]
    bias_qk = [_prompt_bias(b) for b in gb]
    bias_c = [_cache_bias(b, g) for g, b in enumerate(gb)]
    bias_0 = jnp.stack([jnp.broadcast_to(b[0][:, None], (HPG, ATTN_HEAD_DIM)) for b in gb])

    n_c = bp + bd
    rows = -(-n_c // SUBLANES) * SUBLANES
    c_all = jnp.pad(jnp.concatenate([c_prompt, c_sample], axis=0), ((0, rows - n_c), (0, 0)))
    mod_all = _ada(c_all, w_ada, b_ada)

    xp = x_prompt
    xs = x_sample.reshape(1, bd, D_MODEL)
    outs = {k: [] for k in ("kvs0", "kvs1", "kvs2", "convp", "convs", "ssmp")}
    ssm_s = None
    kv_p = None
    for l in range(depth):
        wbs, wba, wo = w_br_ssd[l].astype(BF16), w_br_attn[l].astype(BF16), w_out[l].astype(BF16)
        cw, cb = conv_w[l], conv_b[l].reshape(1, CONV_CH)
        dtb, alog = _lane_pad(dt_bias[l]), _lane_pad(a_log[l])
        dsk_exp = jnp.repeat(d_skip[l].astype(F32), SSD_HEAD_DIM).reshape(1, D_INNER)
        gn = g_ssd_norm[l].reshape(1, D_INNER)
        mod_p = mod_all[l, 0:bp].reshape(bp, 1, 3 * D_MODEL)
        mod_s = mod_all[l, bp:n_c].reshape(1, bd, 3 * D_MODEL)

        zx = _inproj(xp, mod_p, g_pre[l], w_in_t, l, w_last, tm=2048)
        y_p, ssm_p = _ssd_prompt(zx, cw, cb, dtb, alog, dsk_exp, gn, e_sel, tri)
        attn = [_attn_prompt(zx, bias_qk[g], g) for g in range(N_DIL)]
        xp = _outproj(xp, mod_p, y_p, [a[0] for a in attn], [a[1] for a in attn], zx, wbs, wba, wo,
                      g_post[l], tm=512)
        kv_p = _kv_rows(zx, l, depth, kv_p)
        outs["convp"].append(jnp.concatenate(
            [zx[:, seq - (CONV_W - 1):, COL_XS:COL_XS + D_INNER], zx[:, seq - (CONV_W - 1):, COL_B:COL_B + 2048]],
            axis=-1))
        outs["ssmp"].append(ssm_p)

        zs = _inproj(xs, mod_s, g_pre[l], w_in_t, l, w_last, tm=bd)[0]
        new_conv, act, xdt_t, cd = _ssd_pre(zs, conv_states_t, l, cw, cb, dtb, alog, e_sel)
        ssm_s, y_s = _ssd_state(cd[:, 0:N_SSD_HEADS].reshape(-1), state_ssm, l, ssm_s, xdt_t, act, zs,
                                dsk_exp, gn)
        qkv = zs[:, COL_Q:COL_Q + 3 * ATTN_QKV].reshape(bd, 3 * N_DIL, HPG, ATTN_HEAD_DIM)
        o_s = _attn_sample(qkv, caches_t, l, bias_c, bias_0)
        xs = _outproj(xs, mod_s, y_s.reshape(1, bd, D_INNER), [o_s.reshape(1, bd, ATTN_OUT)], [],
                      zs.reshape(1, bd, N_COLS), wbs, wba, wo, g_post[l], tm=bd)
        for g in range(N_DIL):
            kk = zs[:, COL_K + g * ATTN_OUT:COL_K + (g + 1) * ATTN_OUT]
            vv = zs[:, COL_V + g * ATTN_OUT:COL_V + (g + 1) * ATTN_OUT]
            outs[f"kvs{g}"].append(jnp.stack([kk, vv], axis=1).reshape(bd, 1, 2, HPG, ATTN_HEAD_DIM))
        outs["convs"].append(new_conv)

    st = lambda key: jnp.stack(outs[key], axis=0)
    kvp = [jnp.transpose(t, (0, 1, 5, 2, 3, 4)) for t in kv_p]
    return (xp, xs.reshape(bd, 1, D_MODEL), kvp[0], st("kvs0"), kvp[1], st("kvs1"), kvp[2], st("kvs2"),
            st("convp"), jnp.transpose(st("convs"), (0, 2, 1, 3)), st("ssmp"), ssm_s)
```

```python
import functools
import math

import numpy as np
import jax
import jax.numpy as jnp
from jax import lax
from jax.experimental import pallas as pl
from jax.experimental.pallas import tpu as pltpu

F32 = jnp.float32
BF16 = jnp.bfloat16

D_MODEL = 1024
D_INNER = 2048
SSD_HEAD_DIM = 64
N_SSD_HEADS = 32
D_STATE = 128
N_SSD_GROUPS = 8
HEADS_PER_SSD_GROUP = 4
GROUP_CH = D_INNER // N_SSD_GROUPS
CONV_W = 4
CONV_CH = 4096
SSD_CHUNK = 128
DILATION_GROUPS = ((128, 1), (512, 4), (2048, 16))
N_DIL = 3
ATTN_HEAD_DIM = 64
HPG = 12
ATTN_QKV = 2304
ATTN_OUT = 768
ATTN_SCALE = ATTN_HEAD_DIM ** -0.5
N_BUCKETS = 32
MAX_DISTANCE = 2048
RMS_EPS = 1e-6
BAND = 128
MASKED = -1e30

LANES = 128
SUBLANES = 8
VMEM_LIMIT = 56 * 1024 * 1024

COL_Z = 0
COL_XS = 2048
COL_B = 4096
COL_C = 5120
COL_Q = 6144
COL_K = 8448
COL_V = 10752
COL_GA = 13056
COL_MS = 13824
COL_MA = 14848
COL_DT = 15872
N_COLS = 16128
QKV_BLOCKS = N_COLS // ATTN_OUT


def _dot(a, b):
    return jnp.dot(a, b, preferred_element_type=F32)


def _dot_nt(a, b):
    return lax.dot_general(a, b, (((1,), (1,)), ((), ())), preferred_element_type=F32)


def _split3(x):
    x1 = x.astype(BF16)
    r1 = x - x1.astype(F32)
    x2 = r1.astype(BF16)
    x3 = (r1 - x2.astype(F32)).astype(BF16)
    return x1, x2, x3


def _dot_sel_rhs(x, sel):
    x1, x2, x3 = _split3(x)
    return _dot(x1, sel) + _dot(x2, sel) + _dot(x3, sel)


def _dot_sel_lhs(sel, x):
    x1, x2, x3 = _split3(x)
    return _dot(sel, x1) + _dot(sel, x2) + _dot(sel, x3)


def _sigmoid(x):
    return 0.5 + 0.5 * jnp.tanh(0.5 * x)


def _silu(x):
    t = 0.5 * x
    return t + t * jnp.tanh(t)


def _softplus(x):
    return jnp.maximum(x, 0.0) + jnp.log1p(jnp.exp(-jnp.abs(x)))


def _params(*sem):
    return pltpu.CompilerParams(dimension_semantics=sem, vmem_limit_bytes=VMEM_LIMIT)


def _ada_kernel(c_ref, w_ref, b_ref, o_ref):
    c = c_ref[...]
    o_ref[...] = _dot(_silu(c).astype(BF16), w_ref[...].astype(BF16)) + b_ref[...]


def _ada(c_all, w_ada, b_ada):
    depth = w_ada.shape[0]
    rows = c_all.shape[0]
    nj = 3 * D_MODEL // 1024
    return pl.pallas_call(
        _ada_kernel,
        grid=(depth, nj),
        in_specs=[
            pl.BlockSpec((rows, D_MODEL), lambda l, j: (0, 0)),
            pl.BlockSpec((None, D_MODEL, 1024), lambda l, j: (l, 0, j)),
            pl.BlockSpec((None, 1, 1024), lambda l, j: (l, 0, j)),
        ],
        out_specs=pl.BlockSpec((None, rows, 1024), lambda l, j: (l, 0, j)),
        out_shape=jax.ShapeDtypeStruct((depth, rows, 3 * D_MODEL), F32),
        compiler_params=_params("parallel", "parallel"),
        name="ada",
    )(c_all, w_ada, b_ada.reshape(depth, 1, 3 * D_MODEL))


COL_TILE = ATTN_OUT
N_COL_TILES = N_COLS // COL_TILE
DT_ROWS = N_SSD_HEADS
FIRST_TILE_AFTER_DT = COL_Q // COL_TILE
W_IN_ROWS = 15904


def _w_tile_row(j):
    last_start = W_IN_ROWS - COL_TILE
    shifted = j * COL_TILE + jnp.where(j >= FIRST_TILE_AFTER_DT, DT_ROWS, 0)
    return jnp.where(j >= N_COL_TILES - 1, last_start, shifted)


SSD_COL_TILES = COL_Q // COL_TILE


def _chunk_row_tables():
    t = np.arange(SSD_CHUNK)
    row_of_time = (t % 16) * SUBLANES + t // 16
    perm = np.zeros((SSD_CHUNK, SSD_CHUNK), np.float32)
    perm[row_of_time, t] = 1.0
    tri = perm @ np.tril(np.ones((SSD_CHUNK, SSD_CHUNK), np.float32))
    return jnp.asarray(perm, BF16), jnp.asarray(perm.T, BF16), jnp.asarray(tri, BF16)


def _inproj_kernel(permute, x_ref, mod_ref, g_ref, w_ref, wl_ref, p_ref, o_ref, h_ref, *hp):
    j = pl.program_id(2)

    @pl.when(j == 0)
    def _():
        x = x_ref[...]
        ms = jnp.mean(x * x, axis=-1, keepdims=True)
        y = x * lax.rsqrt(ms + RMS_EPS) * g_ref[...]
        shift = mod_ref[:, 0:D_MODEL]
        scale = mod_ref[:, D_MODEL:2 * D_MODEL]
        h_ref[...] = (y * (1.0 + scale) + shift).astype(BF16)
        if permute:
            for c in range(h_ref.shape[0] // SSD_CHUNK):
                rows = slice(c * SSD_CHUNK, (c + 1) * SSD_CHUNK)
                hp[0][rows, :] = _dot(p_ref[...], h_ref[rows, :]).astype(BF16)

    if permute:
        @pl.when(j < SSD_COL_TILES)
        def _():
            o_ref[...] = _dot_nt(hp[0][...], w_ref[0].astype(BF16))

    @pl.when((j >= (SSD_COL_TILES if permute else 0)) & (j < N_COL_TILES - 1))
    def _():
        o_ref[...] = _dot_nt(h_ref[...], w_ref[0].astype(BF16))

    @pl.when(j == N_COL_TILES - 1)
    def _():
        o_ref[...] = _dot_nt(h_ref[...], wl_ref[...].astype(BF16))


def _inproj(x, mod, g_pre, w_t, layer, w_last, tm, chunk_perm=None):
    b, t, _ = x.shape
    tn = COL_TILE
    r = mod.shape[1]
    rb = 1 if r == 1 else tm
    mod_map = (lambda bi, i, j: (bi, 0, 0)) if r == 1 else (lambda bi, i, j: (bi, i, 0))
    permute = chunk_perm is not None
    if not permute:
        chunk_perm = jnp.zeros((SSD_CHUNK, SSD_CHUNK), BF16)
    return pl.pallas_call(
        functools.partial(_inproj_kernel, permute),
        grid=(b, t // tm, N_COL_TILES),
        in_specs=[
            pl.BlockSpec((None, tm, D_MODEL), lambda bi, i, j: (bi, i, 0)),
            pl.BlockSpec((None, rb, 3 * D_MODEL), mod_map),
            pl.BlockSpec((1, D_MODEL), lambda bi, i, j: (0, 0)),
            pl.BlockSpec((pl.Element(1), pl.Element(tn), pl.Element(D_MODEL)),
                         lambda bi, i, j: (layer, pl.multiple_of(_w_tile_row(j), SUBLANES), 0)),
            pl.BlockSpec((None, tn, D_MODEL), lambda bi, i, j: (layer, 0, 0)),
            pl.BlockSpec((SSD_CHUNK, SSD_CHUNK), lambda bi, i, j: (0, 0)),
        ],
        out_specs=pl.BlockSpec((None, tm, tn), lambda bi, i, j: (bi, i, j)),
        out_shape=jax.ShapeDtypeStruct((b, t, N_COLS), F32),
        scratch_shapes=[pltpu.VMEM((tm, D_MODEL), BF16)] * (2 if permute else 1),
        compiler_params=_params("parallel", "parallel", "arbitrary"),
        name="inproj",
    )(x, mod, g_pre.reshape(1, D_MODEL), w_t, w_last, chunk_perm)


def _ssd_prompt_kernel(xs_ref, z_ref, bm_ref, cm_ref, dt_ref, cw_ref, cb_ref, dtb_ref, alog_ref,
                       dsk_ref, gn_ref, e_ref, tri_ref, perm_ref, permt_ref, y_ref, st_out_ref,
                       xpad_ref, tail_ref, act_ref, st_ref, yb_ref):
    c = pl.program_id(1)
    L = SSD_CHUNK
    halo = (CONV_W - 1) * SUBLANES

    @pl.when(c == 0)
    def _():
        tail_ref[...] = jnp.zeros(tail_ref.shape, F32)
        st_ref[...] = jnp.zeros(st_ref.shape, F32)

    xpad_ref[halo:halo + L, 0:D_INNER] = xs_ref[...]
    xpad_ref[halo:halo + L, D_INNER:D_INNER + 1024] = bm_ref[...]
    xpad_ref[halo:halo + L, D_INNER + 1024:CONV_CH] = cm_ref[...]
    sub = lax.broadcasted_iota(jnp.int32, (SUBLANES, CONV_CH), 0)
    for i in range(CONV_W - 1):
        src = L - halo + i * SUBLANES
        cur = xpad_ref[halo + src:halo + src + SUBLANES, :]
        mixed = jnp.where(sub == SUBLANES - 1, tail_ref[i * SUBLANES:(i + 1) * SUBLANES, :], cur)
        xpad_ref[i * SUBLANES:(i + 1) * SUBLANES, :] = pltpu.roll(mixed, 1, axis=0)
    slab = 512
    for s in range(CONV_CH // slab):
        cols = slice(s * slab, (s + 1) * slab)
        acc = cb_ref[:, cols] + cw_ref[CONV_W - 1:CONV_W, cols] * xpad_ref[halo:halo + L, cols]
        for k in range(1, CONV_W):
            lo = halo - k * SUBLANES
            acc = acc + cw_ref[CONV_W - 1 - k:CONV_W - k, cols] * xpad_ref[lo:lo + L, cols]
        act_ref[:, cols] = _silu(acc)
    tail_ref[...] = xpad_ref[L:L + halo, :]

    dt_time = _softplus(dt_ref[...] + dtb_ref[...])
    a = -jnp.exp(alog_ref[...])
    acum = _dot_sel_lhs(tri_ref[...], dt_time * a)
    dt = _dot_sel_lhs(perm_ref[...], dt_time)
    acum_t = acum.T
    dt_t = dt.T
    a_last = acum[L - 1:L, :]
    w_end = dt * jnp.exp(a_last - acum)
    ea = jnp.exp(acum)
    cd = jnp.broadcast_to(jnp.exp(a_last), (SUBLANES, LANES))
    e_sel = e_ref[...]
    w_exp = _dot_sel_rhs(w_end, e_sel)
    ea_exp = _dot_sel_rhs(ea, e_sel)
    cd_exp = _dot_sel_rhs(cd, e_sel)[0:1, :]

    def time_of(r):
        return (r % SUBLANES) * (L // SUBLANES) + r // SUBLANES

    row = lax.broadcasted_iota(jnp.int32, (L, L), 0)
    col = lax.broadcasted_iota(jnp.int32, (L, L), 1)
    causal = time_of(col) <= time_of(row)
    low_half = lax.broadcasted_iota(jnp.int32, (L, LANES), 1) < SSD_HEAD_DIM

    for g in range(N_SSD_GROUPS):
        cols = slice(g * GROUP_CH, (g + 1) * GROUP_CH)
        x_g = act_ref[:, cols]
        b_g = act_ref[:, D_INNER + g * D_STATE:D_INNER + (g + 1) * D_STATE]
        c_g = act_ref[:, D_INNER + 1024 + g * D_STATE:D_INNER + 1024 + (g + 1) * D_STATE]
        xb = x_g.astype(BF16)
        bb = b_g.astype(BF16)
        cbf = c_g.astype(BF16)
        cb = _dot_nt(cbf, bb)
        st = st_ref[g]
        y_g = _dot(cbf, st.astype(BF16)) * ea_exp[:, cols]
        pieces = []
        for pair in range(HEADS_PER_SSD_GROUP // 2):
            xp = xb[:, pair * LANES:(pair + 1) * LANES]
            ys = []
            for k in range(2):
                h = g * HEADS_PER_SSD_GROUP + pair * 2 + k
                seg = acum[:, h:h + 1] - acum_t[h:h + 1, :]
                dec = jnp.exp(jnp.where(causal, seg, -jnp.inf))
                m = (cb * dec * dt_t[h:h + 1, :]).astype(BF16)
                ys.append(_dot(m, xp))
            pieces.append(jnp.where(low_half, ys[0], ys[1]))
        y_g = y_g + jnp.concatenate(pieces, axis=-1) + dsk_ref[:, cols] * x_g
        xw = (x_g * w_exp[:, cols]).astype(BF16)
        st_ref[g] = st * cd_exp[:, cols] + _dot(b_g.T.astype(BF16), xw)
        yz = y_g * _silu(z_ref[:, cols])
        ms = jnp.mean(yz * yz, axis=-1, keepdims=True)
        yb_ref[:, cols] = (yz * lax.rsqrt(ms + RMS_EPS) * gn_ref[:, cols]).astype(BF16)

    y_ref[...] = _dot(permt_ref[...], yb_ref[...]).astype(y_ref.dtype)

    @pl.when(c == pl.num_programs(1) - 1)
    def _():
        for g in range(N_SSD_GROUPS):
            st_out_ref[g * HEADS_PER_SSD_GROUP:(g + 1) * HEADS_PER_SSD_GROUP] = (
                st_ref[g].T.reshape(HEADS_PER_SSD_GROUP, SSD_HEAD_DIM, D_STATE))


def _ssd_prompt(zx, cw, cb, dtb, alog, dsk_exp, gn, e_sel, chunk_tables):
    b, s, _ = zx.shape
    nc = s // SSD_CHUNK
    L = SSD_CHUNK
    halo = (CONV_W - 1) * SUBLANES
    perm, perm_t, tri = chunk_tables
    const = lambda bi, c: (0, 0)
    return pl.pallas_call(
        _ssd_prompt_kernel,
        grid=(b, nc),
        in_specs=[
            pl.BlockSpec((None, L, D_INNER), lambda bi, c: (bi, c, COL_XS // D_INNER)),
            pl.BlockSpec((None, L, D_INNER), lambda bi, c: (bi, c, COL_Z // D_INNER)),
            pl.BlockSpec((None, L, 1024), lambda bi, c: (bi, c, COL_B // 1024)),
            pl.BlockSpec((None, L, 1024), lambda bi, c: (bi, c, COL_C // 1024)),
            pl.BlockSpec((None, L, LANES), lambda bi, c: (bi, c, COL_DT // LANES)),
            pl.BlockSpec((CONV_W, CONV_CH), const),
            pl.BlockSpec((1, CONV_CH), const),
            pl.BlockSpec((1, LANES), const),
            pl.BlockSpec((1, LANES), const),
            pl.BlockSpec((1, D_INNER), const),
            pl.BlockSpec((1, D_INNER), const),
            pl.BlockSpec((LANES, D_INNER), const),
            pl.BlockSpec((L, L), const),
            pl.BlockSpec((L, L), const),
            pl.BlockSpec((L, L), const),
        ],
        out_specs=[
            pl.BlockSpec((None, L, D_INNER), lambda bi, c: (bi, c, 0)),
            pl.BlockSpec((None, N_SSD_HEADS, SSD_HEAD_DIM, D_STATE), lambda bi, c: (bi, 0, 0, 0)),
        ],
        out_shape=[
            jax.ShapeDtypeStruct((b, s, D_INNER), BF16),
            jax.ShapeDtypeStruct((b, N_SSD_HEADS, SSD_HEAD_DIM, D_STATE), F32),
        ],
        scratch_shapes=[
            pltpu.VMEM((halo + L, CONV_CH), F32),
            pltpu.VMEM((halo, CONV_CH), F32),
            pltpu.VMEM((L, CONV_CH), F32),
            pltpu.VMEM((N_SSD_GROUPS, D_STATE, GROUP_CH), F32),
            pltpu.VMEM((L, D_INNER), BF16),
        ],
        compiler_params=_params("parallel", "arbitrary"),
        name="ssd_prompt",
    )(zx, zx, zx, zx, zx, cw, cb, dtb, alog, dsk_exp, gn, e_sel, tri, perm, perm_t)


def _head_pair_attention(q, kb, vb, bias_ref, pair_bias0, no_prev):
    low = lax.broadcasted_iota(jnp.int32, (1, LANES), 1) < ATTN_HEAD_DIM
    if no_prev is not None:
        kj = lax.broadcasted_iota(jnp.int32, (BAND, 2 * BAND), 1)
        hidden = kj < jnp.where(no_prev, BAND, 0)
    os_, ls_ = [], []
    for k in range(2):
        qk = jnp.where(low if k == 0 else jnp.logical_not(low), q, 0.0).astype(BF16)
        s = _dot_nt(qk, kb) + bias_ref[pair_bias0 + k]
        if no_prev is not None:
            s = jnp.where(hidden, MASKED, s)
        m = jnp.max(s, axis=-1, keepdims=True)
        p = jnp.exp(s - m)
        l = jnp.sum(p, axis=-1, keepdims=True)
        os_.append(_dot(p.astype(BF16), vb) / l)
        ls_.append(m + jnp.log(l))
    return jnp.where(low, os_[0], os_[1]), jnp.where(low, ls_[0], ls_[1])


DENSE_BLOCKS = 2


def _attn_dense_kernel(q_ref, kc_ref, vc_ref, kp_ref, vp_ref, bias_ref, o_ref, l_ref):
    first = pl.program_id(1) == 0
    for pair in range(HPG // 2):
        sl = slice(pair * LANES, (pair + 1) * LANES)
        k_all = jnp.concatenate([kp_ref[:, sl], kc_ref[:, sl]], axis=0).astype(BF16)
        v_all = jnp.concatenate([vp_ref[:, sl], vc_ref[:, sl]], axis=0).astype(BF16)
        for i in range(DENSE_BLOCKS):
            rows = slice(i * BAND, (i + 1) * BAND)
            keys = slice(i * BAND, (i + 2) * BAND)
            o, lse = _head_pair_attention(q_ref[rows, sl] * ATTN_SCALE, k_all[keys], v_all[keys], bias_ref,
                                          2 * pair, first if i == 0 else None)
            o_ref[rows, sl] = o
            l_ref[rows, sl] = lse


def _attn_dilated_kernel(dil, q_ref, kc_ref, vc_ref, bias_ref, o_ref, l_ref, kprev_ref, vprev_ref):
    first = pl.program_id(1) == 0
    hp = pl.program_id(2)

    @pl.when(first)
    def _():
        kprev_ref[hp] = jnp.zeros(kprev_ref.shape[1:], BF16)
        vprev_ref[hp] = jnp.zeros(vprev_ref.shape[1:], BF16)

    for r in range(dil):
        rows = pl.ds(r, BAND, stride=dil)
        kc = kc_ref[rows, :].astype(BF16)
        vc = vc_ref[rows, :].astype(BF16)
        kb = jnp.concatenate([kprev_ref[hp, r], kc], axis=0)
        vb = jnp.concatenate([vprev_ref[hp, r], vc], axis=0)
        o, lse = _head_pair_attention(q_ref[rows, :] * ATTN_SCALE, kb, vb, bias_ref, 0, first)
        o_ref[rows, :] = o
        l_ref[rows, :] = lse
        kprev_ref[hp, r] = kc
        vprev_ref[hp, r] = vc


def _attn_prompt(zx, bias_qk, g):
    b, s, _ = zx.shape
    _, dil = DILATION_GROUPS[g]
    span = BAND * dil
    nb = s // span
    out_shape = [jax.ShapeDtypeStruct((b, s, ATTN_OUT), F32)] * 2
    if dil == 1:
        qb, kb, vb = COL_Q // ATTN_OUT + g, COL_K // ATTN_OUT + g, COL_V // ATTN_OUT + g
        blk = (None, DENSE_BLOCKS * BAND, ATTN_OUT)
        pblk = (None, BAND, ATTN_OUT)
        cur = lambda off: (lambda bi, n: (bi, n, off))
        prev = lambda off: (lambda bi, n: (bi, jnp.maximum(DENSE_BLOCKS * n - 1, 0), off))
        out_blk = pl.BlockSpec(blk, lambda bi, n: (bi, n, 0))
        return pl.pallas_call(
            _attn_dense_kernel,
            grid=(b, nb // DENSE_BLOCKS),
            in_specs=[pl.BlockSpec(blk, cur(qb)), pl.BlockSpec(blk, cur(kb)), pl.BlockSpec(blk, cur(vb)),
                      pl.BlockSpec(pblk, prev(kb)), pl.BlockSpec(pblk, prev(vb)),
                      pl.BlockSpec((HPG, BAND, 2 * BAND), lambda bi, n: (0, 0, 0))],
            out_specs=[out_blk, out_blk],
            out_shape=out_shape,
            compiler_params=_params("parallel", "arbitrary"),
            name=f"attn_prompt_g{g}",
        )(zx, zx, zx, zx, zx, bias_qk)
    pairs = HPG // 2
    qb, kb, vb = [(c + g * ATTN_OUT) // LANES for c in (COL_Q, COL_K, COL_V)]
    blk = (None, span, LANES)
    cur = lambda off: (lambda bi, n, hp: (bi, n, off + hp))
    out_blk = pl.BlockSpec(blk, lambda bi, n, hp: (bi, n, hp))
    carry = pltpu.VMEM((pairs, dil, BAND, LANES), BF16)
    return pl.pallas_call(
        functools.partial(_attn_dilated_kernel, dil),
        grid=(b, nb, pairs),
        in_specs=[pl.BlockSpec(blk, cur(qb)), pl.BlockSpec(blk, cur(kb)), pl.BlockSpec(blk, cur(vb)),
                  pl.BlockSpec((2, BAND, 2 * BAND), lambda bi, n, hp: (hp, 0, 0))],
        out_specs=[out_blk, out_blk],
        out_shape=out_shape,
        scratch_shapes=[carry, carry],
        compiler_params=_params("parallel", "arbitrary", "arbitrary"),
        name=f"attn_prompt_g{g}",
    )(zx, zx, zx, bias_qk)


def _kv_rows_kernel(*refs):
    ins, outs = refs[0:2 * N_DIL], refs[-N_DIL:]
    for g in range(N_DIL):
        for kv in range(2):
            t = ins[2 * g + kv][...].T
            outs[g][kv] = t.reshape(2, ATTN_HEAD_DIM, t.shape[-1])


def _kv_rows(zx, layer, depth, stacked):
    b, s, _ = zx.shape
    in_specs, args, out_specs, out_shape = [], [], [], []
    for g, (window, _) in enumerate(DILATION_GROUPS):
        for col in (COL_K, COL_V):
            c0 = (col + g * ATTN_OUT) // LANES
            in_specs.append(pl.BlockSpec((None, window, LANES),
                                         lambda bi, hp, c0=c0, rb=s // window - 1: (bi, rb, c0 + hp)))
            args.append(zx)
        out_specs.append(pl.BlockSpec((None, None, 2, 2, ATTN_HEAD_DIM, window),
                                      lambda bi, hp: (layer, bi, 0, hp, 0, 0)))
        out_shape.append(jax.ShapeDtypeStruct((depth, b, 2, HPG, ATTN_HEAD_DIM, window), F32))
    aliases = {}
    if stacked is not None:
        for g in range(N_DIL):
            in_specs.append(pl.BlockSpec(memory_space=pl.ANY))
            args.append(stacked[g])
            aliases[len(args) - 1] = g
    return pl.pallas_call(
        _kv_rows_kernel,
        grid=(b, HPG // 2),
        in_specs=in_specs,
        out_specs=out_specs,
        out_shape=out_shape,
        input_output_aliases=aliases,
        compiler_params=_params("parallel", "parallel"),
        name="kv_rows",
    )(*args)


def _outproj_kernel(n_groups, *refs):
    x_ref, gate_ref, y_ref = refs[0:3]
    o_refs = refs[3:3 + n_groups]
    l_refs = refs[3 + n_groups:3 + 2 * n_groups] if n_groups > 1 else ()
    rest = refs[3 + (2 * n_groups if n_groups > 1 else 1):]
    ga_ref, ms0_ref, ms1_ref, ma0_ref, ma1_ref, wbs_ref, wba_ref, wo_ref, gp_ref, out_ref = rest

    if n_groups > 1:
        lses = [r[...] for r in l_refs]
        top = functools.reduce(jnp.maximum, lses)
        es = [jnp.exp(l - top) for l in lses]
        num = functools.reduce(lambda u, v: u + v, [e * r[...] for e, r in zip(es, o_refs)])
        o = num / functools.reduce(lambda u, v: u + v, es)
    else:
        o = o_refs[0][...]
    o = o * _silu(ga_ref[...])
    br_s = _dot(y_ref[...].astype(BF16), wbs_ref[...])
    br_a = _dot(o.astype(BF16), wba_ref[...])
    m_ssd = jnp.concatenate([ms0_ref[...], ms1_ref[...]], axis=-1)
    m_attn = jnp.concatenate([ma0_ref[...], ma1_ref[...]], axis=-1)
    merged = _sigmoid(m_ssd) * br_s + _sigmoid(m_attn) * br_a
    out = _dot(merged.astype(BF16), wo_ref[...])
    ms = jnp.mean(out * out, axis=-1, keepdims=True)
    normed = out * lax.rsqrt(ms + RMS_EPS) * gp_ref[...]
    out_ref[...] = x_ref[...] + gate_ref[...] * normed


def _outproj(x, mod, y, os_, ls_, zx, wbs, wba, wo, g_post, tm):
    b, t, _ = x.shape
    r = mod.shape[1]
    rb = 1 if r == 1 else tm
    n_groups = len(os_)
    row = lambda cb_: (lambda bi, i: (bi, i, cb_))
    mod_map = (lambda bi, i: (bi, 0, 2)) if r == 1 else (lambda bi, i: (bi, i, 2))
    const = lambda bi, i: (0, 0)
    in_specs = [
        pl.BlockSpec((None, tm, D_MODEL), row(0)),
        pl.BlockSpec((None, rb, D_MODEL), mod_map),
        pl.BlockSpec((None, tm, D_INNER), row(0)),
    ]
    in_specs += [pl.BlockSpec((None, tm, ATTN_OUT), row(0)) for _ in range(n_groups + len(ls_))]
    in_specs += [
        pl.BlockSpec((None, tm, ATTN_OUT), row(COL_GA // ATTN_OUT)),
        pl.BlockSpec((None, tm, 512), row(COL_MS // 512)),
        pl.BlockSpec((None, tm, 512), row(COL_MS // 512 + 1)),
        pl.BlockSpec((None, tm, 512), row(COL_MA // 512)),
        pl.BlockSpec((None, tm, 512), row(COL_MA // 512 + 1)),
        pl.BlockSpec((D_INNER, D_MODEL), const),
        pl.BlockSpec((ATTN_OUT, D_MODEL), const),
        pl.BlockSpec((D_MODEL, D_MODEL), const),
        pl.BlockSpec((1, D_MODEL), const),
    ]
    return pl.pallas_call(
        functools.partial(_outproj_kernel, n_groups),
        grid=(b, t // tm),
        in_specs=in_specs,
        out_specs=pl.BlockSpec((None, tm, D_MODEL), row(0)),
        out_shape=jax.ShapeDtypeStruct((b, t, D_MODEL), F32),
        compiler_params=_params("parallel", "parallel"),
        name="outproj",
    )(x, mod, y, *os_, *ls_, zx, zx, zx, zx, zx, wbs, wba, wo, g_post.reshape(1, D_MODEL))


def _ssd_pre_kernel(xs_ref, bm_ref, cm_ref, dt_ref, cs_ref, cw_ref, cb_ref, dtb_ref, alog_ref, e_ref,
                    conv_ref, act_ref, xdt_t_ref, cd_ref):
    parts = ((xs_ref, 0, D_INNER), (bm_ref, D_INNER, 1024), (cm_ref, D_INNER + 1024, 1024))
    for ref, off, width in parts:
        cols = slice(off, off + width)
        raw = ref[...]
        acc = cb_ref[:, cols] + cw_ref[CONV_W - 1:CONV_W, cols] * raw
        for w in range(CONV_W - 1):
            acc = acc + cw_ref[w:w + 1, cols] * cs_ref[w, :, cols]
        act_ref[:, cols] = _silu(acc)
        for w in range(1, CONV_W - 1):
            conv_ref[w - 1, :, cols] = cs_ref[w, :, cols]
        conv_ref[CONV_W - 2, :, cols] = raw
    dt = _softplus(dt_ref[...] + dtb_ref[...])
    a = -jnp.exp(alog_ref[...])
    cd_ref[...] = jnp.exp(dt * a)
    xdt = act_ref[:, 0:D_INNER] * _dot_sel_rhs(dt, e_ref[...])
    xdt_t_ref[...] = xdt.T


def _ssd_pre(zs, conv_states_t, layer, cw, cb, dtb, alog, e_sel):
    bd = zs.shape[0]
    const = lambda i: (0, 0)
    taps = (CONV_W - 1, bd, CONV_CH)
    return pl.pallas_call(
        _ssd_pre_kernel,
        grid=(1,),
        in_specs=[
            pl.BlockSpec((bd, D_INNER), lambda i: (0, COL_XS // D_INNER)),
            pl.BlockSpec((bd, 1024), lambda i: (0, COL_B // 1024)),
            pl.BlockSpec((bd, 1024), lambda i: (0, COL_C // 1024)),
            pl.BlockSpec((bd, LANES), lambda i: (0, COL_DT // LANES)),
            pl.BlockSpec((None,) + taps, lambda i: (layer, 0, 0, 0)),
            pl.BlockSpec((CONV_W, CONV_CH), const),
            pl.BlockSpec((1, CONV_CH), const),
            pl.BlockSpec((1, LANES), const),
            pl.BlockSpec((1, LANES), const),
            pl.BlockSpec((LANES, D_INNER), const),
        ],
        out_specs=[
            pl.BlockSpec(taps, lambda i: (0, 0, 0)),
            pl.BlockSpec((bd, CONV_CH), const),
            pl.BlockSpec((D_INNER, bd), const),
            pl.BlockSpec((bd, LANES), const),
        ],
        out_shape=[
            jax.ShapeDtypeStruct(taps, F32),
            jax.ShapeDtypeStruct((bd, CONV_CH), F32),
            jax.ShapeDtypeStruct((D_INNER, bd), F32),
            jax.ShapeDtypeStruct((bd, LANES), F32),
        ],
        compiler_params=_params("arbitrary"),
        name="ssd_pre",
    )(zs, zs, zs, zs, conv_states_t, cw, cb, dtb, alog, e_sel)


SAMPLE_TILE = 8


def _ssd_state_kernel(cd_ref, st_ref, xt_ref, bm_ref, cm_ref, xs_ref, z_ref, dsk_ref, gn_ref, *rest):
    so_ref, y_ref, xb_ref, ya_ref = rest[-4:]
    t = pl.program_id(0)
    ya_ref[...] = jnp.zeros(ya_ref.shape, F32)
    x1, x2, x3 = _split3(xt_ref[...])
    seq = lax.broadcasted_iota(jnp.int32, (xt_ref.shape[1], LANES), 0)
    seq_row = lax.broadcasted_iota(jnp.int32, (SAMPLE_TILE, GROUP_CH), 0)

    for j in range(SAMPLE_TILE):
        pick = jnp.where(seq == t * SAMPLE_TILE + j, 1.0, 0.0).astype(BF16)
        xb_ref[...] = _dot(x1, pick) + _dot(x2, pick) + _dot(x3, pick)

        for g in range(N_SSD_GROUPS):
            lanes = slice(g * D_STATE, (g + 1) * D_STATE)
            brow = bm_ref[j:j + 1, lanes]
            news = []
            for r in range(HEADS_PER_SSD_GROUP):
                h = g * HEADS_PER_SSD_GROUP + r
                cd = cd_ref[(t * SAMPLE_TILE + j) * N_SSD_HEADS + h]
                new = st_ref[j, h] * cd + xb_ref[h * SSD_HEAD_DIM:(h + 1) * SSD_HEAD_DIM, :] * brow
                so_ref[j, h] = new
                news.append(new.astype(BF16))
            yg = _dot_nt(cm_ref[:, lanes].astype(BF16), jnp.concatenate(news, axis=0))
            cols = slice(g * GROUP_CH, (g + 1) * GROUP_CH)
            ya_ref[:, cols] += jnp.where(seq_row == j, yg, 0.0)

    y = ya_ref[...] + dsk_ref[...] * xs_ref[...]
    yz = y * _silu(z_ref[...])
    for g in range(N_SSD_GROUPS):
        cols = slice(g * GROUP_CH, (g + 1) * GROUP_CH)
        v = yz[:, cols]
        ms = jnp.mean(v * v, axis=-1, keepdims=True)
        y_ref[:, cols] = v * lax.rsqrt(ms + RMS_EPS) * gn_ref[:, cols]


def _ssd_state(cd_flat, states, layer, stacked, xdt_t, act, zs, dsk_exp, gn):
    bd = states.shape[1]
    nt = bd // SAMPLE_TILE
    const = lambda t: (0, 0)
    st_blk = (None, SAMPLE_TILE, N_SSD_HEADS, SSD_HEAD_DIM, D_STATE)
    in_specs = [
        pl.BlockSpec(memory_space=pltpu.SMEM),
        pl.BlockSpec(st_blk, lambda t: (layer, t, 0, 0, 0)),
        pl.BlockSpec((D_INNER, bd), const),
        pl.BlockSpec((SAMPLE_TILE, 1024), lambda t: (t, D_INNER // 1024)),
        pl.BlockSpec((SAMPLE_TILE, 1024), lambda t: (t, D_INNER // 1024 + 1)),
        pl.BlockSpec((SAMPLE_TILE, D_INNER), lambda t: (t, 0)),
        pl.BlockSpec((SAMPLE_TILE, D_INNER), lambda t: (t, COL_Z // D_INNER)),
        pl.BlockSpec((1, D_INNER), const),
        pl.BlockSpec((1, D_INNER), const),
    ]
    args = [cd_flat, states, xdt_t, act, act, act, zs, dsk_exp, gn]
    aliases = {}
    if stacked is not None:
        in_specs.append(pl.BlockSpec(memory_space=pl.ANY))
        args.append(stacked)
        aliases = {len(args) - 1: 0}
    return pl.pallas_call(
        _ssd_state_kernel,
        grid=(nt,),
        in_specs=in_specs,
        out_specs=[
            pl.BlockSpec(st_blk, lambda t: (layer, t, 0, 0, 0)),
            pl.BlockSpec((SAMPLE_TILE, D_INNER), lambda t: (t, 0)),
        ],
        out_shape=[
            jax.ShapeDtypeStruct(states.shape, F32),
            jax.ShapeDtypeStruct((bd, D_INNER), F32),
        ],
        scratch_shapes=[pltpu.VMEM((D_INNER, LANES), F32), pltpu.VMEM((SAMPLE_TILE, D_INNER), F32)],
        input_output_aliases=aliases,
        compiler_params=_params("arbitrary"),
        name="ssd_state",
    )(*args)


def _attn_sample_kernel(qkv_ref, c0_ref, c1_ref, c2_ref, bc0_ref, bc1_ref, bc2_ref, b0_ref, o_ref,
                        s_ref, p_ref, t_ref):
    outs, lses = [], []
    for g, (c_ref, bc_ref) in enumerate(((c0_ref, bc0_ref), (c1_ref, bc1_ref), (c2_ref, bc2_ref))):
        wc = c_ref.shape[-1]
        q = qkv_ref[g] * ATTN_SCALE
        kn = qkv_ref[N_DIL + g]
        vn = qkv_ref[2 * N_DIL + g]
        t_ref[...] = jnp.zeros(t_ref.shape, F32)
        t_ref[0:HPG, 0:ATTN_HEAD_DIM] = q
        q_t = t_ref[...].T
        for h in range(HPG):
            qcol = q_t[0:ATTN_HEAD_DIM, h:h + 1]
            s_ref[h:h + 1, 0:wc] = jnp.sum(c_ref[0, h] * qcol, axis=0, keepdims=True)
        s = s_ref[0:HPG, 0:wc] + bc_ref[...]
        sn = jnp.sum(kn * q, axis=-1, keepdims=True) + b0_ref[g][:, 0:1]
        m = jnp.maximum(jnp.max(s, axis=-1, keepdims=True), sn)
        p = jnp.exp(s - m)
        pn = jnp.exp(sn - m)
        l = jnp.sum(p, axis=-1, keepdims=True) + pn
        p_ref[0:HPG, 0:wc] = p
        for h in range(HPG):
            t_ref[0:ATTN_HEAD_DIM, h:h + 1] = jnp.sum(c_ref[1, h] * p_ref[h:h + 1, 0:wc], axis=-1, keepdims=True)
        pv = t_ref[...].T[0:HPG, 0:ATTN_HEAD_DIM]
        outs.append((pv + pn * vn) / l)
        lses.append(m + jnp.log(l))
    top = functools.reduce(jnp.maximum, lses)
    es = [jnp.exp(l - top) for l in lses]
    den = functools.reduce(lambda u, v: u + v, es)
    o_ref[...] = functools.reduce(lambda u, v: u + v, [e * og for e, og in zip(es, outs)]) / den


def _attn_sample(qkv, caches_t, layer, bias_c, bias_0):
    bd = qkv.shape[0]
    specs = [pl.BlockSpec((None, None, 2, HPG, ATTN_HEAD_DIM, c.shape[-1]), lambda b: (layer, b, 0, 0, 0, 0))
             for c in caches_t]
    bias_specs = [pl.BlockSpec(bc.shape, lambda b: (0, 0)) for bc in bias_c]
    wc_max = max(c.shape[-1] for c in caches_t)
    return pl.pallas_call(
        _attn_sample_kernel,
        grid=(bd,),
        in_specs=[
            pl.BlockSpec((None, 3 * N_DIL, HPG, ATTN_HEAD_DIM), lambda b: (b, 0, 0, 0)),
            *specs,
            *bias_specs,
            pl.BlockSpec((N_DIL, HPG, ATTN_HEAD_DIM), lambda b: (0, 0, 0)),
        ],
        out_specs=pl.BlockSpec((None, HPG, ATTN_HEAD_DIM), lambda b: (b, 0, 0)),
        out_shape=jax.ShapeDtypeStruct((bd, HPG, ATTN_HEAD_DIM), F32),
        scratch_shapes=[pltpu.VMEM((2 * SUBLANES, wc_max), F32), pltpu.VMEM((2 * SUBLANES, wc_max), F32),
                        pltpu.VMEM((LANES, LANES), F32)],
        compiler_params=_params("parallel"),
        name="attn_sample",
    )(qkv, *caches_t, *bias_c, bias_0)


def _t5_bucket(dist):
    max_exact = N_BUCKETS // 2
    n = np.maximum(dist, 1).astype(np.float32)
    large = max_exact + (np.log(n / np.float32(max_exact)) / np.float32(math.log(MAX_DISTANCE / max_exact))
                         * np.float32(N_BUCKETS - max_exact)).astype(np.int32)
    large = np.minimum(large, N_BUCKETS - 1)
    return np.where(dist < max_exact, dist, large)


def _select_rows(onehot, table):
    return jnp.dot(jnp.asarray(onehot, F32), table.astype(F32), precision=lax.Precision.HIGHEST)


def _group_bias(rel_bias, g):
    _, dil = DILATION_GROUPS[g]
    bucket = _t5_bucket(np.arange(BAND + 1, dtype=np.int32) * dil)
    onehot = bucket[:, None] == np.arange(N_BUCKETS)[None, :]
    return _select_rows(onehot, rel_bias[:, g * HPG:(g + 1) * HPG])


def _prompt_bias(bias):
    period = 3 * BAND
    row = jnp.concatenate([bias[::-1].T, jnp.full((HPG, period - (BAND + 1)), MASKED, F32)], axis=1)
    flat = jnp.tile(row, (1, BAND))[:, :BAND * (period - 1)]
    return flat.reshape(HPG, BAND, period - 1)[:, :, :2 * BAND]


def _cache_bias(bias, g):
    window, dil = DILATION_GROUPS[g]
    w = np.arange(window)
    step = (window - w) // dil
    on_grid = (w % dil == 0)
    onehot = (np.arange(BAND + 1)[:, None] == step[None, :]) & on_grid[None, :]
    table = jnp.dot(bias.T, jnp.asarray(onehot, F32), precision=lax.Precision.HIGHEST)
    return jnp.where(jnp.asarray(on_grid)[None, :], table, MASKED)


def _last_w_tiles(w_t):
    tail = w_t[:, (N_COL_TILES - 1) * COL_TILE + DT_ROWS:]
    dt = w_t[:, D_INNER + CONV_CH:D_INNER + CONV_CH + DT_ROWS]
    pad = jnp.zeros((w_t.shape[0], COL_TILE - tail.shape[1] - DT_ROWS, D_MODEL), w_t.dtype)
    return jnp.concatenate([tail, dt, pad], axis=1)


def _lane_pad(v):
    return jnp.pad(v.astype(F32), (0, LANES - v.shape[0])).reshape(1, LANES)


def kernel(x_prompt, x_sample, cache_kv_w128, cache_kv_w512, cache_kv_w2048, state_conv, state_ssm,
           c_prompt, c_sample, rel_bias, w_ada, b_ada, g_pre, w_in, conv_w, conv_b, dt_bias, a_log,
           d_skip, g_ssd_norm, w_br_ssd, w_br_attn, w_out, g_post):
    depth = w_in.shape[0]
    bp, seq, _ = x_prompt.shape
    bd = x_sample.shape[0]
    assert x_sample.shape[1] == 1 and seq % (BAND * 16) == 0 and bd % SAMPLE_TILE == 0 and bd % LANES == 0
    caches = (cache_kv_w128, cache_kv_w512, cache_kv_w2048)
    assert all(c.shape[2] == w for c, (w, _) in zip(caches, DILATION_GROUPS))
    caches_t = [jnp.transpose(c, (0, 1, 3, 4, 5, 2)) for c in caches]
    assert w_in.shape[-1] == W_IN_ROWS
    w_in_t = jnp.swapaxes(w_in, 1, 2)
    w_last = _last_w_tiles(w_in_t)
    conv_states_t = jnp.transpose(state_conv, (0, 2, 1, 3))

    head_of_ch = np.arange(D_INNER) // SSD_HEAD_DIM
    e_sel = jnp.asarray(head_of_ch[None, :] == np.arange(LANES)[:, None], BF16)
    chunk_tables = _chunk_row_tables()
    last_times = np.arange(SSD_CHUNK - (CONV_W - 1), SSD_CHUNK)
    last_rows = [int(r) for r in (last_times % 16) * SUBLANES + last_times // 16]

    gb = [_group_bias(rel_bias, g) for g in range(N_DIL)]
    bias_qk = [_prompt_bias(b) for b in gb]
    bias_c = [_cache_bias(b, g) for g, b in enumerate(gb)]
    bias_0 = jnp.stack([jnp.broadcast_to(b[0][:, None], (HPG, ATTN_HEAD_DIM)) for b in gb])

    n_c = bp + bd
    rows = -(-n_c // SUBLANES) * SUBLANES
    c_all = jnp.pad(jnp.concatenate([c_prompt, c_sample], axis=0), ((0, rows - n_c), (0, 0)))
    mod_all = _ada(c_all, w_ada, b_ada)

    xp = x_prompt
    xs = x_sample.reshape(1, bd, D_MODEL)
    outs = {k: [] for k in ("kvs0", "kvs1", "kvs2", "convp", "convs", "ssmp")}
    ssm_s = None
    kv_p = None
    for l in range(depth):
        wbs, wba, wo = w_br_ssd[l].astype(BF16), w_br_attn[l].astype(BF16), w_out[l].astype(BF16)
        cw, cb = conv_w[l], conv_b[l].reshape(1, CONV_CH)
        dtb, alog = _lane_pad(dt_bias[l]), _lane_pad(a_log[l])
        dsk_exp = jnp.repeat(d_skip[l].astype(F32), SSD_HEAD_DIM).reshape(1, D_INNER)
        gn = g_ssd_norm[l].reshape(1, D_INNER)
        mod_p = mod_all[l, 0:bp].reshape(bp, 1, 3 * D_MODEL)
        mod_s = mod_all[l, bp:n_c].reshape(1, bd, 3 * D_MODEL)

        zx = _inproj(xp, mod_p, g_pre[l], w_in_t, l, w_last, tm=2048, chunk_perm=chunk_tables[0])
        y_p, ssm_p = _ssd_prompt(zx, cw, cb, dtb, alog, dsk_exp, gn, e_sel, chunk_tables)
        attn = [_attn_prompt(zx, bias_qk[g], g) for g in range(N_DIL)]
        xp = _outproj(xp, mod_p, y_p, [a[0] for a in attn], [a[1] for a in attn], zx, wbs, wba, wo,
                      g_post[l], tm=512)
        kv_p = _kv_rows(zx, l, depth, kv_p)
        last_chunk = zx[:, seq - SSD_CHUNK:, COL_XS:COL_XS + CONV_CH]
        outs["convp"].append(jnp.stack([last_chunk[:, r] for r in last_rows], axis=1))
        outs["ssmp"].append(ssm_p)

        zs = _inproj(xs, mod_s, g_pre[l], w_in_t, l, w_last, tm=bd)[0]
        new_conv, act, xdt_t, cd = _ssd_pre(zs, conv_states_t, l, cw, cb, dtb, alog, e_sel)
        ssm_s, y_s = _ssd_state(cd[:, 0:N_SSD_HEADS].reshape(-1), state_ssm, l, ssm_s, xdt_t, act, zs,
                                dsk_exp, gn)
        qkv = zs[:, COL_Q:COL_Q + 3 * ATTN_QKV].reshape(bd, 3 * N_DIL, HPG, ATTN_HEAD_DIM)
        o_s = _attn_sample(qkv, caches_t, l, bias_c, bias_0)
        xs = _outproj(xs, mod_s, y_s.reshape(1, bd, D_INNER), [o_s.reshape(1, bd, ATTN_OUT)], [],
                      zs.reshape(1, bd, N_COLS), wbs, wba, wo, g_post[l], tm=bd)
        for g in range(N_DIL):
            kk = zs[:, COL_K + g * ATTN_OUT:COL_K + (g + 1) * ATTN_OUT]
            vv = zs[:, COL_V + g * ATTN_OUT:COL_V + (g + 1) * ATTN_OUT]
            outs[f"kvs{g}"].append(jnp.stack([kk, vv], axis=1).reshape(bd, 1, 2, HPG, ATTN_HEAD_DIM))
        outs["convs"].append(new_conv)

    st = lambda key: jnp.stack(outs[key], axis=0)
    kvp = [jnp.transpose(t, (0, 1, 5, 2, 3, 4)) for t in kv_p]
    return (xp, xs.reshape(bd, 1, D_MODEL), kvp[0], st("kvs0"), kvp[1], st("kvs1"), kvp[2], st("kvs2"),
            st("convp"), jnp.transpose(st("convs"), (0, 2, 1, 3)), st("ssmp"), ssm_s)
```

```python
import functools
import math

import numpy as np
import jax
import jax.numpy as jnp
from jax import lax
from jax.experimental import pallas as pl
from jax.experimental.pallas import tpu as pltpu

F32 = jnp.float32
BF16 = jnp.bfloat16

D_MODEL = 1024
D_INNER = 2048
SSD_HEAD_DIM = 64
N_SSD_HEADS = 32
D_STATE = 128
N_SSD_GROUPS = 8
HEADS_PER_SSD_GROUP = 4
GROUP_CH = D_INNER // N_SSD_GROUPS
CONV_W = 4
CONV_CH = 4096
SSD_CHUNK = 128
DILATION_GROUPS = ((128, 1), (512, 4), (2048, 16))
N_DIL = 3
ATTN_HEAD_DIM = 64
HPG = 12
ATTN_QKV = 2304
ATTN_OUT = 768
ATTN_SCALE = ATTN_HEAD_DIM ** -0.5
N_BUCKETS = 32
MAX_DISTANCE = 2048
RMS_EPS = 1e-6
BAND = 128
MASKED = -1e30

LANES = 128
SUBLANES = 8
VMEM_LIMIT = 56 * 1024 * 1024

COL_Z = 0
COL_XS = 2048
COL_B = 4096
COL_C = 5120
COL_Q = 6144
COL_K = 8448
COL_V = 10752
COL_GA = 13056
COL_MS = 13824
COL_MA = 14848
COL_DT = 15872
N_COLS = 16128
QKV_BLOCKS = N_COLS // ATTN_OUT


def _dot(a, b):
    return jnp.dot(a, b, preferred_element_type=F32)


def _dot_nt(a, b):
    return lax.dot_general(a, b, (((1,), (1,)), ((), ())), preferred_element_type=F32)


def _split3(x):
    x1 = x.astype(BF16)
    r1 = x - x1.astype(F32)
    x2 = r1.astype(BF16)
    x3 = (r1 - x2.astype(F32)).astype(BF16)
    return x1, x2, x3


def _dot_sel_rhs(x, sel):
    x1, x2, x3 = _split3(x)
    return _dot(x1, sel) + _dot(x2, sel) + _dot(x3, sel)


def _dot_sel_lhs(sel, x):
    x1, x2, x3 = _split3(x)
    return _dot(sel, x1) + _dot(sel, x2) + _dot(sel, x3)


def _sigmoid(x):
    return 0.5 + 0.5 * jnp.tanh(0.5 * x)


def _silu(x):
    t = 0.5 * x
    return t + t * jnp.tanh(t)


def _softplus(x):
    return jnp.maximum(x, 0.0) + jnp.log1p(jnp.exp(-jnp.abs(x)))


def _params(*sem):
    return pltpu.CompilerParams(dimension_semantics=sem, vmem_limit_bytes=VMEM_LIMIT)


def _ada_kernel(c_ref, w_ref, b_ref, o_ref):
    c = c_ref[...]
    o_ref[...] = _dot(_silu(c).astype(BF16), w_ref[...].astype(BF16)) + b_ref[...]


def _ada(c_all, w_ada, b_ada):
    depth = w_ada.shape[0]
    rows = c_all.shape[0]
    nj = 3 * D_MODEL // 1024
    return pl.pallas_call(
        _ada_kernel,
        grid=(depth, nj),
        in_specs=[
            pl.BlockSpec((rows, D_MODEL), lambda l, j: (0, 0)),
            pl.BlockSpec((None, D_MODEL, 1024), lambda l, j: (l, 0, j)),
            pl.BlockSpec((None, 1, 1024), lambda l, j: (l, 0, j)),
        ],
        out_specs=pl.BlockSpec((None, rows, 1024), lambda l, j: (l, 0, j)),
        out_shape=jax.ShapeDtypeStruct((depth, rows, 3 * D_MODEL), F32),
        compiler_params=_params("parallel", "parallel"),
        name="ada",
    )(c_all, w_ada, b_ada.reshape(depth, 1, 3 * D_MODEL))


COL_TILE = ATTN_OUT
N_COL_TILES = N_COLS // COL_TILE
DT_ROWS = N_SSD_HEADS
FIRST_TILE_AFTER_DT = COL_Q // COL_TILE
W_IN_ROWS = 15904


def _w_tile_row(j):
    last_start = W_IN_ROWS - COL_TILE
    shifted = j * COL_TILE + jnp.where(j >= FIRST_TILE_AFTER_DT, DT_ROWS, 0)
    return jnp.where(j >= N_COL_TILES - 1, last_start, shifted)


SSD_COL_TILES = COL_Q // COL_TILE


def _chunk_row_tables():
    t = np.arange(SSD_CHUNK)
    row_of_time = (t % 16) * SUBLANES + t // 16
    perm = np.zeros((SSD_CHUNK, SSD_CHUNK), np.float32)
    perm[row_of_time, t] = 1.0
    tri = perm @ np.tril(np.ones((SSD_CHUNK, SSD_CHUNK), np.float32))
    return jnp.asarray(perm, BF16), jnp.asarray(perm.T, BF16), jnp.asarray(tri, BF16)


def _inproj_kernel(permute, x_ref, mod_ref, g_ref, w_ref, wl_ref, p_ref, o_ref, h_ref, *hp):
    j = pl.program_id(2)

    @pl.when(j == 0)
    def _():
        x = x_ref[...]
        ms = jnp.mean(x * x, axis=-1, keepdims=True)
        y = x * lax.rsqrt(ms + RMS_EPS) * g_ref[...]
        shift = mod_ref[:, 0:D_MODEL]
        scale = mod_ref[:, D_MODEL:2 * D_MODEL]
        h_ref[...] = (y * (1.0 + scale) + shift).astype(BF16)
        if permute:
            for c in range(h_ref.shape[0] // SSD_CHUNK):
                rows = slice(c * SSD_CHUNK, (c + 1) * SSD_CHUNK)
                hp[0][rows, :] = _dot(p_ref[...], h_ref[rows, :]).astype(BF16)

    if permute:
        @pl.when(j < SSD_COL_TILES)
        def _():
            o_ref[...] = _dot_nt(hp[0][...], w_ref[0].astype(BF16))

    @pl.when((j >= (SSD_COL_TILES if permute else 0)) & (j < N_COL_TILES - 1))
    def _():
        o_ref[...] = _dot_nt(h_ref[...], w_ref[0].astype(BF16))

    @pl.when(j == N_COL_TILES - 1)
    def _():
        o_ref[...] = _dot_nt(h_ref[...], wl_ref[...].astype(BF16))


def _inproj(x, mod, g_pre, w_t, layer, w_last, tm, chunk_perm=None):
    b, t, _ = x.shape
    tn = COL_TILE
    r = mod.shape[1]
    rb = 1 if r == 1 else tm
    mod_map = (lambda bi, i, j: (bi, 0, 0)) if r == 1 else (lambda bi, i, j: (bi, i, 0))
    permute = chunk_perm is not None
    if not permute:
        chunk_perm = jnp.zeros((SSD_CHUNK, SSD_CHUNK), BF16)
    return pl.pallas_call(
        functools.partial(_inproj_kernel, permute),
        grid=(b, t // tm, N_COL_TILES),
        in_specs=[
            pl.BlockSpec((None, tm, D_MODEL), lambda bi, i, j: (bi, i, 0)),
            pl.BlockSpec((None, rb, 3 * D_MODEL), mod_map),
            pl.BlockSpec((1, D_MODEL), lambda bi, i, j: (0, 0)),
            pl.BlockSpec((pl.Element(1), pl.Element(tn), pl.Element(D_MODEL)),
                         lambda bi, i, j: (layer, pl.multiple_of(_w_tile_row(j), SUBLANES), 0)),
            pl.BlockSpec((None, tn, D_MODEL), lambda bi, i, j: (layer, 0, 0)),
            pl.BlockSpec((SSD_CHUNK, SSD_CHUNK), lambda bi, i, j: (0, 0)),
        ],
        out_specs=pl.BlockSpec((None, tm, tn), lambda bi, i, j: (bi, i, j)),
        out_shape=jax.ShapeDtypeStruct((b, t, N_COLS), F32),
        scratch_shapes=[pltpu.VMEM((tm, D_MODEL), BF16)] * (2 if permute else 1),
        compiler_params=_params("parallel", "parallel", "arbitrary"),
        name="inproj",
    )(x, mod, g_pre.reshape(1, D_MODEL), w_t, w_last, chunk_perm)


CHUNKS_PER_STEP = 4


def _ssd_prompt_kernel(xs_ref, z_ref, bm_ref, cm_ref, dt_ref, cw_ref, cb_ref, dtb_ref, alog_ref,
                       dsk_ref, gn_ref, e_ref, tri_ref, perm_ref, permt_ref, y_ref, st_out_ref,
                       xpad_ref, tail_ref, act_ref, st_ref, yb_ref):
    c = pl.program_id(1)

    @pl.when(c == 0)
    def _():
        tail_ref[...] = jnp.zeros(tail_ref.shape, F32)
        st_ref[...] = jnp.zeros(st_ref.shape, F32)

    for ci in range(CHUNKS_PER_STEP):
        rows = pl.ds(ci * SSD_CHUNK, SSD_CHUNK)
        _ssd_chunk(xs_ref.at[rows], z_ref.at[rows], bm_ref.at[rows], cm_ref.at[rows], dt_ref.at[rows],
                   cw_ref, cb_ref, dtb_ref, alog_ref, dsk_ref, gn_ref, e_ref, tri_ref, perm_ref, permt_ref,
                   y_ref.at[rows], xpad_ref.at[ci], tail_ref, act_ref.at[ci], st_ref, yb_ref.at[ci])

    @pl.when(c == pl.num_programs(1) - 1)
    def _():
        for g in range(N_SSD_GROUPS):
            st_out_ref[g * HEADS_PER_SSD_GROUP:(g + 1) * HEADS_PER_SSD_GROUP] = (
                st_ref[g].T.reshape(HEADS_PER_SSD_GROUP, SSD_HEAD_DIM, D_STATE))


def _ssd_chunk(xs_ref, z_ref, bm_ref, cm_ref, dt_ref, cw_ref, cb_ref, dtb_ref, alog_ref,
               dsk_ref, gn_ref, e_ref, tri_ref, perm_ref, permt_ref, y_ref,
               xpad_ref, tail_ref, act_ref, st_ref, yb_ref):
    L = SSD_CHUNK
    halo = (CONV_W - 1) * SUBLANES

    xpad_ref[halo:halo + L, 0:D_INNER] = xs_ref[...]
    xpad_ref[halo:halo + L, D_INNER:D_INNER + 1024] = bm_ref[...]
    xpad_ref[halo:halo + L, D_INNER + 1024:CONV_CH] = cm_ref[...]
    sub = lax.broadcasted_iota(jnp.int32, (SUBLANES, CONV_CH), 0)
    for i in range(CONV_W - 1):
        src = L - halo + i * SUBLANES
        cur = xpad_ref[halo + src:halo + src + SUBLANES, :]
        mixed = jnp.where(sub == SUBLANES - 1, tail_ref[i * SUBLANES:(i + 1) * SUBLANES, :], cur)
        xpad_ref[i * SUBLANES:(i + 1) * SUBLANES, :] = pltpu.roll(mixed, 1, axis=0)
    slab = 512
    for s in range(CONV_CH // slab):
        cols = slice(s * slab, (s + 1) * slab)
        acc = cb_ref[:, cols] + cw_ref[CONV_W - 1:CONV_W, cols] * xpad_ref[halo:halo + L, cols]
        for k in range(1, CONV_W):
            lo = halo - k * SUBLANES
            acc = acc + cw_ref[CONV_W - 1 - k:CONV_W - k, cols] * xpad_ref[lo:lo + L, cols]
        act_ref[:, cols] = _silu(acc)
    tail_ref[...] = xpad_ref[L:L + halo, :]

    dt_time = _softplus(dt_ref[...] + dtb_ref[...])
    a = -jnp.exp(alog_ref[...])
    acum = _dot_sel_lhs(tri_ref[...], dt_time * a)
    dt = _dot_sel_lhs(perm_ref[...], dt_time)
    acum_t = acum.T
    dt_t = dt.T
    a_last = acum[L - 1:L, :]
    w_end = dt * jnp.exp(a_last - acum)
    ea = jnp.exp(acum)
    cd = jnp.broadcast_to(jnp.exp(a_last), (SUBLANES, LANES))
    e_sel = e_ref[...]
    w_exp = _dot_sel_rhs(w_end, e_sel)
    ea_exp = _dot_sel_rhs(ea, e_sel)
    cd_exp = _dot_sel_rhs(cd, e_sel)[0:1, :]

    def time_of(r):
        return (r % SUBLANES) * (L // SUBLANES) + r // SUBLANES

    row = lax.broadcasted_iota(jnp.int32, (L, L), 0)
    col = lax.broadcasted_iota(jnp.int32, (L, L), 1)
    causal = time_of(col) <= time_of(row)
    low_half = lax.broadcasted_iota(jnp.int32, (L, LANES), 1) < SSD_HEAD_DIM

    for g in range(N_SSD_GROUPS):
        cols = slice(g * GROUP_CH, (g + 1) * GROUP_CH)
        x_g = act_ref[:, cols]
        b_g = act_ref[:, D_INNER + g * D_STATE:D_INNER + (g + 1) * D_STATE]
        c_g = act_ref[:, D_INNER + 1024 + g * D_STATE:D_INNER + 1024 + (g + 1) * D_STATE]
        xb = x_g.astype(BF16)
        bb = b_g.astype(BF16)
        cbf = c_g.astype(BF16)
        cb = _dot_nt(cbf, bb)
        st = st_ref[g]
        y_g = _dot(cbf, st.astype(BF16)) * ea_exp[:, cols]
        pieces = []
        for pair in range(HEADS_PER_SSD_GROUP // 2):
            xp = xb[:, pair * LANES:(pair + 1) * LANES]
            ys = []
            for k in range(2):
                h = g * HEADS_PER_SSD_GROUP + pair * 2 + k
                seg = acum[:, h:h + 1] - acum_t[h:h + 1, :]
                dec = jnp.exp(jnp.where(causal, seg, -jnp.inf))
                m = (cb * dec * dt_t[h:h + 1, :]).astype(BF16)
                ys.append(_dot(m, xp))
            pieces.append(jnp.where(low_half, ys[0], ys[1]))
        y_g = y_g + jnp.concatenate(pieces, axis=-1) + dsk_ref[:, cols] * x_g
        xw = (x_g * w_exp[:, cols]).astype(BF16)
        st_ref[g] = st * cd_exp[:, cols] + _dot(b_g.T.astype(BF16), xw)
        yz = y_g * _silu(z_ref[:, cols])
        ms = jnp.mean(yz * yz, axis=-1, keepdims=True)
        yb_ref[:, cols] = (yz * lax.rsqrt(ms + RMS_EPS) * gn_ref[:, cols]).astype(BF16)

    y_ref[...] = _dot(permt_ref[...], yb_ref[...]).astype(y_ref.dtype)


def _ssd_prompt(zx, cw, cb, dtb, alog, dsk_exp, gn, e_sel, chunk_tables):
    b, s, _ = zx.shape
    nc = s // SSD_CHUNK
    L = SSD_CHUNK
    halo = (CONV_W - 1) * SUBLANES
    perm, perm_t, tri = chunk_tables
    const = lambda bi, c: (0, 0)
    n_sub = CHUNKS_PER_STEP
    rows = n_sub * L
    return pl.pallas_call(
        _ssd_prompt_kernel,
        grid=(b, nc // n_sub),
        in_specs=[
            pl.BlockSpec((None, rows, D_INNER), lambda bi, c: (bi, c, COL_XS // D_INNER)),
            pl.BlockSpec((None, rows, D_INNER), lambda bi, c: (bi, c, COL_Z // D_INNER)),
            pl.BlockSpec((None, rows, 1024), lambda bi, c: (bi, c, COL_B // 1024)),
            pl.BlockSpec((None, rows, 1024), lambda bi, c: (bi, c, COL_C // 1024)),
            pl.BlockSpec((None, rows, LANES), lambda bi, c: (bi, c, COL_DT // LANES)),
            pl.BlockSpec((CONV_W, CONV_CH), const),
            pl.BlockSpec((1, CONV_CH), const),
            pl.BlockSpec((1, LANES), const),
            pl.BlockSpec((1, LANES), const),
            pl.BlockSpec((1, D_INNER), const),
            pl.BlockSpec((1, D_INNER), const),
            pl.BlockSpec((LANES, D_INNER), const),
            pl.BlockSpec((L, L), const),
            pl.BlockSpec((L, L), const),
            pl.BlockSpec((L, L), const),
        ],
        out_specs=[
            pl.BlockSpec((None, rows, D_INNER), lambda bi, c: (bi, c, 0)),
            pl.BlockSpec((None, N_SSD_HEADS, SSD_HEAD_DIM, D_STATE), lambda bi, c: (bi, 0, 0, 0)),
        ],
        out_shape=[
            jax.ShapeDtypeStruct((b, s, D_INNER), BF16),
            jax.ShapeDtypeStruct((b, N_SSD_HEADS, SSD_HEAD_DIM, D_STATE), F32),
        ],
        scratch_shapes=[
            pltpu.VMEM((n_sub, halo + L, CONV_CH), F32),
            pltpu.VMEM((halo, CONV_CH), F32),
            pltpu.VMEM((n_sub, L, CONV_CH), F32),
            pltpu.VMEM((N_SSD_GROUPS, D_STATE, GROUP_CH), F32),
            pltpu.VMEM((n_sub, L, D_INNER), BF16),
        ],
        compiler_params=_params("parallel", "arbitrary"),
        name="ssd_prompt",
    )(zx, zx, zx, zx, zx, cw, cb, dtb, alog, dsk_exp, gn, e_sel, tri, perm, perm_t)


def _head_pair_attention(q, kb, vb, bias_ref, pair_bias0, no_prev):
    low = lax.broadcasted_iota(jnp.int32, (1, LANES), 1) < ATTN_HEAD_DIM
    if no_prev is not None:
        kj = lax.broadcasted_iota(jnp.int32, (BAND, 2 * BAND), 1)
        hidden = kj < jnp.where(no_prev, BAND, 0)
    os_, ls_ = [], []
    for k in range(2):
        qk = jnp.where(low if k == 0 else jnp.logical_not(low), q, 0.0).astype(BF16)
        s = _dot_nt(qk, kb) + bias_ref[pair_bias0 + k]
        if no_prev is not None:
            s = jnp.where(hidden, MASKED, s)
        m = jnp.max(s, axis=-1, keepdims=True)
        p = jnp.exp(s - m)
        l = jnp.sum(p, axis=-1, keepdims=True)
        os_.append(_dot(p.astype(BF16), vb) / l)
        ls_.append(m + jnp.log(l))
    return jnp.where(low, os_[0], os_[1]), jnp.where(low, ls_[0], ls_[1])


DENSE_BLOCKS = 4


def _attn_dense_kernel(q_ref, kc_ref, vc_ref, kp_ref, vp_ref, bias_ref, o_ref, l_ref):
    first = pl.program_id(1) == 0
    for pair in range(HPG // 2):
        sl = slice(pair * LANES, (pair + 1) * LANES)
        k_all = jnp.concatenate([kp_ref[:, sl], kc_ref[:, sl]], axis=0).astype(BF16)
        v_all = jnp.concatenate([vp_ref[:, sl], vc_ref[:, sl]], axis=0).astype(BF16)
        for i in range(DENSE_BLOCKS):
            rows = slice(i * BAND, (i + 1) * BAND)
            keys = slice(i * BAND, (i + 2) * BAND)
            o, lse = _head_pair_attention(q_ref[rows, sl] * ATTN_SCALE, k_all[keys], v_all[keys], bias_ref,
                                          2 * pair, first if i == 0 else None)
            o_ref[rows, sl] = o
            l_ref[rows, sl] = lse


def _attn_dilated_kernel(dil, q_ref, kc_ref, vc_ref, bias_ref, o_ref, l_ref, kprev_ref, vprev_ref):
    first = pl.program_id(1) == 0
    hp = pl.program_id(2)

    @pl.when(first)
    def _():
        kprev_ref[hp] = jnp.zeros(kprev_ref.shape[1:], BF16)
        vprev_ref[hp] = jnp.zeros(vprev_ref.shape[1:], BF16)

    for r in range(dil):
        rows = pl.ds(r, BAND, stride=dil)
        kc = kc_ref[rows, :].astype(BF16)
        vc = vc_ref[rows, :].astype(BF16)
        kb = jnp.concatenate([kprev_ref[hp, r], kc], axis=0)
        vb = jnp.concatenate([vprev_ref[hp, r], vc], axis=0)
        o, lse = _head_pair_attention(q_ref[rows, :] * ATTN_SCALE, kb, vb, bias_ref, 0, first)
        o_ref[rows, :] = o
        l_ref[rows, :] = lse
        kprev_ref[hp, r] = kc
        vprev_ref[hp, r] = vc


def _attn_prompt(zx, bias_qk, g):
    b, s, _ = zx.shape
    _, dil = DILATION_GROUPS[g]
    span = BAND * dil
    nb = s // span
    out_shape = [jax.ShapeDtypeStruct((b, s, ATTN_OUT), F32)] * 2
    if dil == 1:
        qb, kb, vb = COL_Q // ATTN_OUT + g, COL_K // ATTN_OUT + g, COL_V // ATTN_OUT + g
        blk = (None, DENSE_BLOCKS * BAND, ATTN_OUT)
        pblk = (None, BAND, ATTN_OUT)
        cur = lambda off: (lambda bi, n: (bi, n, off))
        prev = lambda off: (lambda bi, n: (bi, jnp.maximum(DENSE_BLOCKS * n - 1, 0), off))
        out_blk = pl.BlockSpec(blk, lambda bi, n: (bi, n, 0))
        return pl.pallas_call(
            _attn_dense_kernel,
            grid=(b, nb // DENSE_BLOCKS),
            in_specs=[pl.BlockSpec(blk, cur(qb)), pl.BlockSpec(blk, cur(kb)), pl.BlockSpec(blk, cur(vb)),
                      pl.BlockSpec(pblk, prev(kb)), pl.BlockSpec(pblk, prev(vb)),
                      pl.BlockSpec((HPG, BAND, 2 * BAND), lambda bi, n: (0, 0, 0))],
            out_specs=[out_blk, out_blk],
            out_shape=out_shape,
            compiler_params=_params("parallel", "arbitrary"),
            name=f"attn_prompt_g{g}",
        )(zx, zx, zx, zx, zx, bias_qk)
    pairs = HPG // 2
    qb, kb, vb = [(c + g * ATTN_OUT) // LANES for c in (COL_Q, COL_K, COL_V)]
    blk = (None, span, LANES)
    cur = lambda off: (lambda bi, n, hp: (bi, n, off + hp))
    out_blk = pl.BlockSpec(blk, lambda bi, n, hp: (bi, n, hp))
    carry = pltpu.VMEM((pairs, dil, BAND, LANES), BF16)
    return pl.pallas_call(
        functools.partial(_attn_dilated_kernel, dil),
        grid=(b, nb, pairs),
        in_specs=[pl.BlockSpec(blk, cur(qb)), pl.BlockSpec(blk, cur(kb)), pl.BlockSpec(blk, cur(vb)),
                  pl.BlockSpec((2, BAND, 2 * BAND), lambda bi, n, hp: (hp, 0, 0))],
        out_specs=[out_blk, out_blk],
        out_shape=out_shape,
        scratch_shapes=[carry, carry],
        compiler_params=_params("parallel", "arbitrary", "arbitrary"),
        name=f"attn_prompt_g{g}",
    )(zx, zx, zx, bias_qk)


def _kv_rows_kernel(*refs):
    ins, outs = refs[0:2 * N_DIL], refs[-N_DIL:]
    for g in range(N_DIL):
        for kv in range(2):
            t = ins[2 * g + kv][...].T
            outs[g][kv] = t.reshape(2, ATTN_HEAD_DIM, t.shape[-1])


def _kv_rows(zx, layer, depth, stacked):
    b, s, _ = zx.shape
    in_specs, args, out_specs, out_shape = [], [], [], []
    for g, (window, _) in enumerate(DILATION_GROUPS):
        for col in (COL_K, COL_V):
            c0 = (col + g * ATTN_OUT) // LANES
            in_specs.append(pl.BlockSpec((None, window, LANES),
                                         lambda bi, hp, c0=c0, rb=s // window - 1: (bi, rb, c0 + hp)))
            args.append(zx)
        out_specs.append(pl.BlockSpec((None, None, 2, 2, ATTN_HEAD_DIM, window),
                                      lambda bi, hp: (layer, bi, 0, hp, 0, 0)))
        out_shape.append(jax.ShapeDtypeStruct((depth, b, 2, HPG, ATTN_HEAD_DIM, window), F32))
    aliases = {}
    if stacked is not None:
        for g in range(N_DIL):
            in_specs.append(pl.BlockSpec(memory_space=pl.ANY))
            args.append(stacked[g])
            aliases[len(args) - 1] = g
    return pl.pallas_call(
        _kv_rows_kernel,
        grid=(b, HPG // 2),
        in_specs=in_specs,
        out_specs=out_specs,
        out_shape=out_shape,
        input_output_aliases=aliases,
        compiler_params=_params("parallel", "parallel"),
        name="kv_rows",
    )(*args)


def _outproj_kernel(n_groups, *refs):
    x_ref, gate_ref, y_ref = refs[0:3]
    o_refs = refs[3:3 + n_groups]
    l_refs = refs[3 + n_groups:3 + 2 * n_groups] if n_groups > 1 else ()
    rest = refs[3 + (2 * n_groups if n_groups > 1 else 1):]
    ga_ref, ms0_ref, ms1_ref, ma0_ref, ma1_ref, wbs_ref, wba_ref, wo_ref, gp_ref, out_ref = rest

    if n_groups > 1:
        lses = [r[...] for r in l_refs]
        top = functools.reduce(jnp.maximum, lses)
        es = [jnp.exp(l - top) for l in lses]
        num = functools.reduce(lambda u, v: u + v, [e * r[...] for e, r in zip(es, o_refs)])
        o = num / functools.reduce(lambda u, v: u + v, es)
    else:
        o = o_refs[0][...]
    o = o * _silu(ga_ref[...])
    br_s = _dot(y_ref[...].astype(BF16), wbs_ref[...])
    br_a = _dot(o.astype(BF16), wba_ref[...])
    m_ssd = jnp.concatenate([ms0_ref[...], ms1_ref[...]], axis=-1)
    m_attn = jnp.concatenate([ma0_ref[...], ma1_ref[...]], axis=-1)
    merged = _sigmoid(m_ssd) * br_s + _sigmoid(m_attn) * br_a
    out = _dot(merged.astype(BF16), wo_ref[...])
    ms = jnp.mean(out * out, axis=-1, keepdims=True)
    normed = out * lax.rsqrt(ms + RMS_EPS) * gp_ref[...]
    out_ref[...] = x_ref[...] + gate_ref[...] * normed


def _outproj(x, mod, y, os_, ls_, zx, wbs, wba, wo, g_post, tm):
    b, t, _ = x.shape
    r = mod.shape[1]
    rb = 1 if r == 1 else tm
    n_groups = len(os_)
    row = lambda cb_: (lambda bi, i: (bi, i, cb_))
    mod_map = (lambda bi, i: (bi, 0, 2)) if r == 1 else (lambda bi, i: (bi, i, 2))
    const = lambda bi, i: (0, 0)
    in_specs = [
        pl.BlockSpec((None, tm, D_MODEL), row(0)),
        pl.BlockSpec((None, rb, D_MODEL), mod_map),
        pl.BlockSpec((None, tm, D_INNER), row(0)),
    ]
    in_specs += [pl.BlockSpec((None, tm, ATTN_OUT), row(0)) for _ in range(n_groups + len(ls_))]
    in_specs += [
        pl.BlockSpec((None, tm, ATTN_OUT), row(COL_GA // ATTN_OUT)),
        pl.BlockSpec((None, tm, 512), row(COL_MS // 512)),
        pl.BlockSpec((None, tm, 512), row(COL_MS // 512 + 1)),
        pl.BlockSpec((None, tm, 512), row(COL_MA // 512)),
        pl.BlockSpec((None, tm, 512), row(COL_MA // 512 + 1)),
        pl.BlockSpec((D_INNER, D_MODEL), const),
        pl.BlockSpec((ATTN_OUT, D_MODEL), const),
        pl.BlockSpec((D_MODEL, D_MODEL), const),
        pl.BlockSpec((1, D_MODEL), const),
    ]
    return pl.pallas_call(
        functools.partial(_outproj_kernel, n_groups),
        grid=(b, t // tm),
        in_specs=in_specs,
        out_specs=pl.BlockSpec((None, tm, D_MODEL), row(0)),
        out_shape=jax.ShapeDtypeStruct((b, t, D_MODEL), F32),
        compiler_params=_params("parallel", "parallel"),
        name="outproj",
    )(x, mod, y, *os_, *ls_, zx, zx, zx, zx, zx, wbs, wba, wo, g_post.reshape(1, D_MODEL))


def _ssd_pre_kernel(xs_ref, bm_ref, cm_ref, dt_ref, cs_ref, cw_ref, cb_ref, dtb_ref, alog_ref, e_ref,
                    conv_ref, act_ref, xdt_t_ref, cd_ref):
    parts = ((xs_ref, 0, D_INNER), (bm_ref, D_INNER, 1024), (cm_ref, D_INNER + 1024, 1024))
    for ref, off, width in parts:
        cols = slice(off, off + width)
        raw = ref[...]
        acc = cb_ref[:, cols] + cw_ref[CONV_W - 1:CONV_W, cols] * raw
        for w in range(CONV_W - 1):
            acc = acc + cw_ref[w:w + 1, cols] * cs_ref[w, :, cols]
        act_ref[:, cols] = _silu(acc)
        for w in range(1, CONV_W - 1):
            conv_ref[w - 1, :, cols] = cs_ref[w, :, cols]
        conv_ref[CONV_W - 2, :, cols] = raw
    dt = _softplus(dt_ref[...] + dtb_ref[...])
    a = -jnp.exp(alog_ref[...])
    cd_ref[...] = jnp.exp(dt * a)
    xdt = act_ref[:, 0:D_INNER] * _dot_sel_rhs(dt, e_ref[...])
    xdt_t_ref[...] = xdt.T


def _ssd_pre(zs, conv_states_t, layer, cw, cb, dtb, alog, e_sel):
    bd = zs.shape[0]
    const = lambda i: (0, 0)
    taps = (CONV_W - 1, bd, CONV_CH)
    return pl.pallas_call(
        _ssd_pre_kernel,
        grid=(1,),
        in_specs=[
            pl.BlockSpec((bd, D_INNER), lambda i: (0, COL_XS // D_INNER)),
            pl.BlockSpec((bd, 1024), lambda i: (0, COL_B // 1024)),
            pl.BlockSpec((bd, 1024), lambda i: (0, COL_C // 1024)),
            pl.BlockSpec((bd, LANES), lambda i: (0, COL_DT // LANES)),
            pl.BlockSpec((None,) + taps, lambda i: (layer, 0, 0, 0)),
            pl.BlockSpec((CONV_W, CONV_CH), const),
            pl.BlockSpec((1, CONV_CH), const),
            pl.BlockSpec((1, LANES), const),
            pl.BlockSpec((1, LANES), const),
            pl.BlockSpec((LANES, D_INNER), const),
        ],
        out_specs=[
            pl.BlockSpec(taps, lambda i: (0, 0, 0)),
            pl.BlockSpec((bd, CONV_CH), const),
            pl.BlockSpec((D_INNER, bd), const),
            pl.BlockSpec((bd, LANES), const),
        ],
        out_shape=[
            jax.ShapeDtypeStruct(taps, F32),
            jax.ShapeDtypeStruct((bd, CONV_CH), F32),
            jax.ShapeDtypeStruct((D_INNER, bd), F32),
            jax.ShapeDtypeStruct((bd, LANES), F32),
        ],
        compiler_params=_params("arbitrary"),
        name="ssd_pre",
    )(zs, zs, zs, zs, conv_states_t, cw, cb, dtb, alog, e_sel)


SAMPLE_TILE = 8


def _ssd_state_kernel(cd_ref, st_ref, xt_ref, bm_ref, cm_ref, xs_ref, z_ref, dsk_ref, gn_ref, *rest):
    so_ref, y_ref, xb_ref, ya_ref = rest[-4:]
    t = pl.program_id(0)
    ya_ref[...] = jnp.zeros(ya_ref.shape, F32)
    x1, x2, x3 = _split3(xt_ref[...])
    seq = lax.broadcasted_iota(jnp.int32, (xt_ref.shape[1], LANES), 0)
    seq_row = lax.broadcasted_iota(jnp.int32, (SAMPLE_TILE, GROUP_CH), 0)

    for j in range(SAMPLE_TILE):
        pick = jnp.where(seq == t * SAMPLE_TILE + j, 1.0, 0.0).astype(BF16)
        xb_ref[...] = _dot(x1, pick) + _dot(x2, pick) + _dot(x3, pick)

        for g in range(N_SSD_GROUPS):
            lanes = slice(g * D_STATE, (g + 1) * D_STATE)
            brow = bm_ref[j:j + 1, lanes]
            news = []
            for r in range(HEADS_PER_SSD_GROUP):
                h = g * HEADS_PER_SSD_GROUP + r
                cd = cd_ref[(t * SAMPLE_TILE + j) * N_SSD_HEADS + h]
                new = st_ref[j, h] * cd + xb_ref[h * SSD_HEAD_DIM:(h + 1) * SSD_HEAD_DIM, :] * brow
                so_ref[j, h] = new
                news.append(new.astype(BF16))
            yg = _dot_nt(cm_ref[:, lanes].astype(BF16), jnp.concatenate(news, axis=0))
            cols = slice(g * GROUP_CH, (g + 1) * GROUP_CH)
            ya_ref[:, cols] += jnp.where(seq_row == j, yg, 0.0)

    y = ya_ref[...] + dsk_ref[...] * xs_ref[...]
    yz = y * _silu(z_ref[...])
    for g in range(N_SSD_GROUPS):
        cols = slice(g * GROUP_CH, (g + 1) * GROUP_CH)
        v = yz[:, cols]
        ms = jnp.mean(v * v, axis=-1, keepdims=True)
        y_ref[:, cols] = v * lax.rsqrt(ms + RMS_EPS) * gn_ref[:, cols]


def _ssd_state(cd_flat, states, layer, stacked, xdt_t, act, zs, dsk_exp, gn):
    bd = states.shape[1]
    nt = bd // SAMPLE_TILE
    const = lambda t: (0, 0)
    st_blk = (None, SAMPLE_TILE, N_SSD_HEADS, SSD_HEAD_DIM, D_STATE)
    in_specs = [
        pl.BlockSpec(memory_space=pltpu.SMEM),
        pl.BlockSpec(st_blk, lambda t: (layer, t, 0, 0, 0)),
        pl.BlockSpec((D_INNER, bd), const),
        pl.BlockSpec((SAMPLE_TILE, 1024), lambda t: (t, D_INNER // 1024)),
        pl.BlockSpec((SAMPLE_TILE, 1024), lambda t: (t, D_INNER // 1024 + 1)),
        pl.BlockSpec((SAMPLE_TILE, D_INNER), lambda t: (t, 0)),
        pl.BlockSpec((SAMPLE_TILE, D_INNER), lambda t: (t, COL_Z // D_INNER)),
        pl.BlockSpec((1, D_INNER), const),
        pl.BlockSpec((1, D_INNER), const),
    ]
    args = [cd_flat, states, xdt_t, act, act, act, zs, dsk_exp, gn]
    aliases = {}
    if stacked is not None:
        in_specs.append(pl.BlockSpec(memory_space=pl.ANY))
        args.append(stacked)
        aliases = {len(args) - 1: 0}
    return pl.pallas_call(
        _ssd_state_kernel,
        grid=(nt,),
        in_specs=in_specs,
        out_specs=[
            pl.BlockSpec(st_blk, lambda t: (layer, t, 0, 0, 0)),
            pl.BlockSpec((SAMPLE_TILE, D_INNER), lambda t: (t, 0)),
        ],
        out_shape=[
            jax.ShapeDtypeStruct(states.shape, F32),
            jax.ShapeDtypeStruct((bd, D_INNER), F32),
        ],
        scratch_shapes=[pltpu.VMEM((D_INNER, LANES), F32), pltpu.VMEM((SAMPLE_TILE, D_INNER), F32)],
        input_output_aliases=aliases,
        compiler_params=_params("arbitrary"),
        name="ssd_state",
    )(*args)


def _attn_sample_kernel(qkv_ref, c0_ref, c1_ref, c2_ref, bc0_ref, bc1_ref, bc2_ref, b0_ref, o_ref,
                        s_ref, p_ref, t_ref):
    outs, lses = [], []
    for g, (c_ref, bc_ref) in enumerate(((c0_ref, bc0_ref), (c1_ref, bc1_ref), (c2_ref, bc2_ref))):
        wc = c_ref.shape[-1]
        q = qkv_ref[g] * ATTN_SCALE
        kn = qkv_ref[N_DIL + g]
        vn = qkv_ref[2 * N_DIL + g]
        t_ref[...] = jnp.zeros(t_ref.shape, F32)
        t_ref[0:HPG, 0:ATTN_HEAD_DIM] = q
        q_t = t_ref[...].T
        for h in range(HPG):
            qcol = q_t[0:ATTN_HEAD_DIM, h:h + 1]
            s_ref[h:h + 1, 0:wc] = jnp.sum(c_ref[0, h] * qcol, axis=0, keepdims=True)
        s = s_ref[0:HPG, 0:wc] + bc_ref[...]
        sn = jnp.sum(kn * q, axis=-1, keepdims=True) + b0_ref[g][:, 0:1]
        m = jnp.maximum(jnp.max(s, axis=-1, keepdims=True), sn)
        p = jnp.exp(s - m)
        pn = jnp.exp(sn - m)
        l = jnp.sum(p, axis=-1, keepdims=True) + pn
        p_ref[0:HPG, 0:wc] = p
        for h in range(HPG):
            t_ref[0:ATTN_HEAD_DIM, h:h + 1] = jnp.sum(c_ref[1, h] * p_ref[h:h + 1, 0:wc], axis=-1, keepdims=True)
        pv = t_ref[...].T[0:HPG, 0:ATTN_HEAD_DIM]
        outs.append((pv + pn * vn) / l)
        lses.append(m + jnp.log(l))
    top = functools.reduce(jnp.maximum, lses)
    es = [jnp.exp(l - top) for l in lses]
    den = functools.reduce(lambda u, v: u + v, es)
    o_ref[...] = functools.reduce(lambda u, v: u + v, [e * og for e, og in zip(es, outs)]) / den


def _attn_sample(qkv, caches_t, layer, bias_c, bias_0):
    bd = qkv.shape[0]
    specs = [pl.BlockSpec((None, None, 2, HPG, ATTN_HEAD_DIM, c.shape[-1]), lambda b: (layer, b, 0, 0, 0, 0))
             for c in caches_t]
    bias_specs = [pl.BlockSpec(bc.shape, lambda b: (0, 0)) for bc in bias_c]
    wc_max = max(c.shape[-1] for c in caches_t)
    return pl.pallas_call(
        _attn_sample_kernel,
        grid=(bd,),
        in_specs=[
            pl.BlockSpec((None, 3 * N_DIL, HPG, ATTN_HEAD_DIM), lambda b: (b, 0, 0, 0)),
            *specs,
            *bias_specs,
            pl.BlockSpec((N_DIL, HPG, ATTN_HEAD_DIM), lambda b: (0, 0, 0)),
        ],
        out_specs=pl.BlockSpec((None, HPG, ATTN_HEAD_DIM), lambda b: (b, 0, 0)),
        out_shape=jax.ShapeDtypeStruct((bd, HPG, ATTN_HEAD_DIM), F32),
        scratch_shapes=[pltpu.VMEM((2 * SUBLANES, wc_max), F32), pltpu.VMEM((2 * SUBLANES, wc_max), F32),
                        pltpu.VMEM((LANES, LANES), F32)],
        compiler_params=_params("parallel"),
        name="attn_sample",
    )(qkv, *caches_t, *bias_c, bias_0)


def _t5_bucket(dist):
    max_exact = N_BUCKETS // 2
    n = np.maximum(dist, 1).astype(np.float32)
    large = max_exact + (np.log(n / np.float32(max_exact)) / np.float32(math.log(MAX_DISTANCE / max_exact))
                         * np.float32(N_BUCKETS - max_exact)).astype(np.int32)
    large = np.minimum(large, N_BUCKETS - 1)
    return np.where(dist < max_exact, dist, large)


def _select_rows(onehot, table):
    return jnp.dot(jnp.asarray(onehot, F32), table.astype(F32), precision=lax.Precision.HIGHEST)


def _group_bias(rel_bias, g):
    _, dil = DILATION_GROUPS[g]
    bucket = _t5_bucket(np.arange(BAND + 1, dtype=np.int32) * dil)
    onehot = bucket[:, None] == np.arange(N_BUCKETS)[None, :]
    return _select_rows(onehot, rel_bias[:, g * HPG:(g + 1) * HPG])


def _prompt_bias(bias):
    period = 3 * BAND
    row = jnp.concatenate([bias[::-1].T, jnp.full((HPG, period - (BAND + 1)), MASKED, F32)], axis=1)
    flat = jnp.tile(row, (1, BAND))[:, :BAND * (period - 1)]
    return flat.reshape(HPG, BAND, period - 1)[:, :, :2 * BAND]


def _cache_bias(bias, g):
    window, dil = DILATION_GROUPS[g]
    w = np.arange(window)
    step = (window - w) // dil
    on_grid = (w % dil == 0)
    onehot = (np.arange(BAND + 1)[:, None] == step[None, :]) & on_grid[None, :]
    table = jnp.dot(bias.T, jnp.asarray(onehot, F32), precision=lax.Precision.HIGHEST)
    return jnp.where(jnp.asarray(on_grid)[None, :], table, MASKED)


def _last_w_tiles(w_t):
    tail = w_t[:, (N_COL_TILES - 1) * COL_TILE + DT_ROWS:]
    dt = w_t[:, D_INNER + CONV_CH:D_INNER + CONV_CH + DT_ROWS]
    pad = jnp.zeros((w_t.shape[0], COL_TILE - tail.shape[1] - DT_ROWS, D_MODEL), w_t.dtype)
    return jnp.concatenate([tail, dt, pad], axis=1)


def _lane_pad(v):
    return jnp.pad(v.astype(F32), (0, LANES - v.shape[0])).reshape(1, LANES)


def kernel(x_prompt, x_sample, cache_kv_w128, cache_kv_w512, cache_kv_w2048, state_conv, state_ssm,
           c_prompt, c_sample, rel_bias, w_ada, b_ada, g_pre, w_in, conv_w, conv_b, dt_bias, a_log,
           d_skip, g_ssd_norm, w_br_ssd, w_br_attn, w_out, g_post):
    depth = w_in.shape[0]
    bp, seq, _ = x_prompt.shape
    bd = x_sample.shape[0]
    assert x_sample.shape[1] == 1 and seq % (BAND * 16) == 0 and bd % SAMPLE_TILE == 0 and bd % LANES == 0
    caches = (cache_kv_w128, cache_kv_w512, cache_kv_w2048)
    assert all(c.shape[2] == w for c, (w, _) in zip(caches, DILATION_GROUPS))
    caches_t = [jnp.transpose(c, (0, 1, 3, 4, 5, 2)) for c in caches]
    assert w_in.shape[-1] == W_IN_ROWS
    w_in_t = jnp.swapaxes(w_in, 1, 2)
    w_last = _last_w_tiles(w_in_t)
    conv_states_t = jnp.transpose(state_conv, (0, 2, 1, 3))

    head_of_ch = np.arange(D_INNER) // SSD_HEAD_DIM
    e_sel = jnp.asarray(head_of_ch[None, :] == np.arange(LANES)[:, None], BF16)
    chunk_tables = _chunk_row_tables()
    last_times = np.arange(SSD_CHUNK - (CONV_W - 1), SSD_CHUNK)
    last_rows = [int(r) for r in (last_times % 16) * SUBLANES + last_times // 16]

    gb = [_group_bias(rel_bias, g) for g in range(N_DIL)]
    bias_qk = [_prompt_bias(b) for b in gb]
    bias_c = [_cache_bias(b, g) for g, b in enumerate(gb)]
    bias_0 = jnp.stack([jnp.broadcast_to(b[0][:, None], (HPG, ATTN_HEAD_DIM)) for b in gb])

    n_c = bp + bd
    rows = -(-n_c // SUBLANES) * SUBLANES
    c_all = jnp.pad(jnp.concatenate([c_prompt, c_sample], axis=0), ((0, rows - n_c), (0, 0)))
    mod_all = _ada(c_all, w_ada, b_ada)

    xp = x_prompt
    xs = x_sample.reshape(1, bd, D_MODEL)
    outs = {k: [] for k in ("kvs0", "kvs1", "kvs2", "convp", "convs", "ssmp")}
    ssm_s = None
    kv_p = None
    for l in range(depth):
        wbs, wba, wo = w_br_ssd[l].astype(BF16), w_br_attn[l].astype(BF16), w_out[l].astype(BF16)
        cw, cb = conv_w[l], conv_b[l].reshape(1, CONV_CH)
        dtb, alog = _lane_pad(dt_bias[l]), _lane_pad(a_log[l])
        dsk_exp = jnp.repeat(d_skip[l].astype(F32), SSD_HEAD_DIM).reshape(1, D_INNER)
        gn = g_ssd_norm[l].reshape(1, D_INNER)
        mod_p = mod_all[l, 0:bp].reshape(bp, 1, 3 * D_MODEL)
        mod_s = mod_all[l, bp:n_c].reshape(1, bd, 3 * D_MODEL)

        zx = _inproj(xp, mod_p, g_pre[l], w_in_t, l, w_last, tm=2048, chunk_perm=chunk_tables[0])
        y_p, ssm_p = _ssd_prompt(zx, cw, cb, dtb, alog, dsk_exp, gn, e_sel, chunk_tables)
        attn = [_attn_prompt(zx, bias_qk[g], g) for g in range(N_DIL)]
        xp = _outproj(xp, mod_p, y_p, [a[0] for a in attn], [a[1] for a in attn], zx, wbs, wba, wo,
                      g_post[l], tm=512)
        kv_p = _kv_rows(zx, l, depth, kv_p)
        last_chunk = zx[:, seq - SSD_CHUNK:, COL_XS:COL_XS + CONV_CH]
        outs["convp"].append(jnp.stack([last_chunk[:, r] for r in last_rows], axis=1))
        outs["ssmp"].append(ssm_p)

        zs = _inproj(xs, mod_s, g_pre[l], w_in_t, l, w_last, tm=bd)[0]
        new_conv, act, xdt_t, cd = _ssd_pre(zs, conv_states_t, l, cw, cb, dtb, alog, e_sel)
        ssm_s, y_s = _ssd_state(cd[:, 0:N_SSD_HEADS].reshape(-1), state_ssm, l, ssm_s, xdt_t, act, zs,
                                dsk_exp, gn)
        qkv = zs[:, COL_Q:COL_Q + 3 * ATTN_QKV].reshape(bd, 3 * N_DIL, HPG, ATTN_HEAD_DIM)
        o_s = _attn_sample(qkv, caches_t, l, bias_c, bias_0)
        xs = _outproj(xs, mod_s, y_s.reshape(1, bd, D_INNER), [o_s.reshape(1, bd, ATTN_OUT)], [],
                      zs.reshape(1, bd, N_COLS), wbs, wba, wo, g_post[l], tm=bd)
        for g in range(N_DIL):
            kk = zs[:, COL_K + g * ATTN_OUT:COL_K + (g + 1) * ATTN_OUT]
            vv = zs[:, COL_V + g * ATTN_OUT:COL_V + (g + 1) * ATTN_OUT]
            outs[f"kvs{g}"].append(jnp.stack([kk, vv], axis=1).reshape(bd, 1, 2, HPG, ATTN_HEAD_DIM))
        outs["convs"].append(new_conv)

    st = lambda key: jnp.stack(outs[key], axis=0)
    kvp = [jnp.transpose(t, (0, 1, 5, 2, 3, 4)) for t in kv_p]
    return (xp, xs.reshape(bd, 1, D_MODEL), kvp[0], st("kvs0"), kvp[1], st("kvs1"), kvp[2], st("kvs2"),
            st("convp"), jnp.transpose(st("convs"), (0, 2, 1, 3)), st("ssmp"), ssm_s)
```
